```python
import numpy as np
import jax
import jax.numpy as jnp
from jax import lax

D_MODEL = 1024
BATCH = 8
SEQ = 4096
DEPTH = 2

HEAD_DIM = 64
ROPE_THETA = 10000.0
RMS_EPS = 1e-6
GN_EPS = 1e-5
RWKV_LN_EPS = 64e-5
D_FF = 2816
N_BRANCH = 4
BAND_BLOCK = 128
NEG_INF = -1e30
FORCED_SCORE = 1e9

NSA_HEADS = 4
NSA_KV_DIM = HEAD_DIM
NSA_CMP_LEN = 32
NSA_CMP_STRIDE = 16
NSA_CMP_HIDDEN = 256
NSA_SEL_BLOCK = 64
NSA_TOP_N = 16
NSA_WINDOW = 512

RET_HEADS = 4
RET_DK = HEAD_DIM
RET_DV = 2 * HEAD_DIM
RET_CHUNK = 128

RWKV_HEADS = 4
RWKV_W_LORA = 64
RWKV_A_LORA = 64
RWKV_G_LORA = 128

SWA_HEADS = 4
SWA_KV_HEADS = 2
SWA_WINDOW = 128

NSA_WIDTH = NSA_HEADS * HEAD_DIM
RET_WIDTH = RET_HEADS * RET_DV
RWKV_WIDTH = RWKV_HEADS * HEAD_DIM
SWA_WIDTH = SWA_HEADS * HEAD_DIM
RWKV_MIX_WIDTH = 3 * RWKV_WIDTH + RWKV_W_LORA + RWKV_A_LORA + RWKV_G_LORA

kernel_name = 'hybrid_nsa_retnet_rwkv7_swa_macaron'


def _column_layout():
    spec = (
        ('nsa_q', NSA_WIDTH), ('nsa_k_cmp', NSA_KV_DIM), ('nsa_v_cmp', NSA_KV_DIM),
        ('nsa_k_slc', NSA_KV_DIM), ('nsa_v_slc', NSA_KV_DIM),
        ('nsa_k_win', NSA_KV_DIM), ('nsa_v_win', NSA_KV_DIM), ('nsa_gate', 3 * NSA_HEADS),
        ('ret_q', RET_HEADS * RET_DK), ('ret_k', RET_HEADS * RET_DK),
        ('ret_v', RET_WIDTH), ('ret_g', RET_WIDTH),
        ('rwkv', RWKV_MIX_WIDTH),
        ('swa_q', SWA_WIDTH), ('swa_k', SWA_KV_HEADS * HEAD_DIM), ('swa_v', SWA_KV_HEADS * HEAD_DIM),
        ('branch_gate', N_BRANCH * D_MODEL),
    )
    layout, start = {}, 0
    for name, width in spec:
        layout[name] = (start, start + width)
        start += width
    return layout, start


def _cols(p, layout, name):
    s, e = layout[name]
    return p[..., s:e]


def rms_norm(x, g):
    xf = x.astype(jnp.float32)
    y = xf * lax.rsqrt(jnp.mean(xf * xf, axis=-1, keepdims=True) + RMS_EPS)
    return (y * g.astype(jnp.float32)).astype(x.dtype)


def swiglu(x, w_gate, w_up, w_down):
    return (jax.nn.silu(x @ w_gate) * (x @ w_up)) @ w_down


def rope(t, pos):
    d = t.shape[-1]
    half = d // 2
    inv_freq = jnp.power(ROPE_THETA, -jnp.arange(half, dtype=jnp.float32) * 2.0 / d)
    ang = pos.astype(jnp.float32)[:, None] * inv_freq[None, :]
    cos = jnp.cos(ang)[:, None, :]
    sin = jnp.sin(ang)[:, None, :]
    tf = t.astype(jnp.float32)
    t1, t2 = tf[..., :half], tf[..., half:]
    return jnp.concatenate([t1 * cos - t2 * sin, t2 * cos + t1 * sin], axis=-1).astype(t.dtype)


def masked_softmax(s, mask):
    s = jnp.where(mask, s.astype(jnp.float32), NEG_INF)
    m = jnp.max(s, axis=-1, keepdims=True)
    p = jnp.exp(s - m) * mask
    return p / jnp.maximum(jnp.sum(p, axis=-1, keepdims=True), 1e-30)


def banded_attention(q, k, v, window, sinks=None):
    B, S, Hk, G, D = q.shape
    blk = BAND_BLOCK
    nb = S // blk
    n_prev = -(-(window - 1) // blk)
    pad = n_prev * blk
    n_keys = (n_prev + 1) * blk
    kp = jnp.pad(k, ((0, 0), (pad, 0), (0, 0), (0, 0))).reshape(B, nb + n_prev, blk, Hk, D)
    vp = jnp.pad(v, ((0, 0), (pad, 0), (0, 0), (0, 0))).reshape(B, nb + n_prev, blk, Hk, D)
    kb = jnp.concatenate([kp[:, j:j + nb] for j in range(n_prev + 1)], axis=2)
    vb = jnp.concatenate([vp[:, j:j + nb] for j in range(n_prev + 1)], axis=2)
    qb = q.reshape(B, nb, blk, Hk, G, D)
    s = jnp.einsum('bnqhgd,bnkhd->bnhgqk', qb, kb).astype(jnp.float32) * (D ** -0.5)
    qpos = jnp.arange(nb)[:, None] * blk + jnp.arange(blk)[None, :]
    kpos = jnp.arange(nb)[:, None] * blk - pad + jnp.arange(n_keys)[None, :]
    diff = qpos[:, :, None] - kpos[:, None, :]
    mask = ((diff >= 0) & (diff < window) & (kpos[:, None, :] >= 0))[None, :, None, None]
    s = jnp.where(mask, s, NEG_INF)
    m = jnp.max(s, axis=-1, keepdims=True)
    if sinks is not None:
        sk = sinks.astype(jnp.float32)[None, None, :, :, None, None]
        m = jnp.maximum(m, sk)
    p = jnp.exp(s - m) * mask
    denom = jnp.sum(p, axis=-1, keepdims=True)
    if sinks is not None:
        denom = denom + jnp.exp(sk - m)
    p = (p / denom).astype(v.dtype)
    o = jnp.einsum('bnhgqk,bnkhd->bnqhgd', p, vb)
    return o.reshape(B, S, Hk * G * D)


def nsa_mixer(q, k_cmp, v_cmp, k_slc, v_slc, k_win, v_win, gate_logits,
              pos_k, pos_v, ck_w1, ck_w2, cv_w1, cv_w2, pos):
    B, S, H, D = q.shape
    scale = D ** -0.5
    n_cmp = (S - NSA_CMP_LEN) // NSA_CMP_STRIDE + 1
    cmp_start = jnp.arange(n_cmp) * NSA_CMP_STRIDE
    cmp_idx = cmp_start[:, None] + jnp.arange(NSA_CMP_LEN)[None, :]

    def compress(t, pos_emb, w1, w2):
        blocks = (t[:, cmp_idx] + pos_emb).reshape(B, n_cmp, NSA_CMP_LEN * D)
        return jax.nn.gelu(blocks @ w1) @ w2

    kc = compress(k_cmp, pos_k, ck_w1, ck_w2)
    vc = compress(v_cmp, pos_v, cv_w1, cv_w2)
    cmp_mask = (cmp_start + NSA_CMP_LEN - 1)[None, :] <= pos[:, None]
    p_cmp = masked_softmax(jnp.einsum('bshd,bnd->bhsn', q, kc) * scale, cmp_mask)
    o_cmp = jnp.einsum('bhsn,bnd->bshd', p_cmp.astype(vc.dtype), vc)
    n_sel = S // NSA_SEL_BLOCK
    sel_start = jnp.arange(n_sel) * NSA_SEL_BLOCK
    overlap = ((cmp_start[:, None] <= sel_start[None, :] + NSA_SEL_BLOCK - 1)
               & (cmp_start[:, None] + NSA_CMP_LEN - 1 >= sel_start[None, :])).astype(jnp.float32)
    importance = jnp.einsum('bhsn,nj->bsj', p_cmp, overlap)
    cur = pos // NSA_SEL_BLOCK
    blk_id = jnp.arange(n_sel)
    valid = blk_id[None, :] <= cur[:, None]
    forced = (blk_id[None, :] == 0) | (blk_id[None, :] == cur[:, None]) | (blk_id[None, :] == cur[:, None] - 1)
    score = jnp.where(forced, FORCED_SCORE, jnp.where(valid, importance, -FORCED_SCORE))
    n_top = min(NSA_TOP_N, n_sel)
    _, sel = lax.top_k(score, n_top)
    sel_valid = sel <= cur[None, :, None]
    q_rot = rope(q, pos)
    k_slc = rope(k_slc[:, :, None], pos)[:, :, 0]
    k_win = rope(k_win[:, :, None], pos)[:, :, 0]
    blk = BAND_BLOCK
    nqb = S // blk

    def sel_block(args):
        qb, selb, validb, posb = args
        tok4 = selb[..., None] * NSA_SEL_BLOCK + jnp.arange(NSA_SEL_BLOCK)
        mask = (validb[..., None] & (tok4 <= posb[None, :, None, None])).reshape(B, blk, -1)
        tok = tok4.reshape(B, blk, -1)
        ks = jax.vmap(lambda t, i: t[i])(k_slc, tok)
        vs = jax.vmap(lambda t, i: t[i])(v_slc, tok)
        p = masked_softmax(jnp.einsum('bqhd,bqtd->bhqt', qb, ks) * scale, mask[:, None])
        return jnp.einsum('bhqt,bqtd->bqhd', p.astype(vs.dtype), vs)

    def to_blocks(t):
        return jnp.swapaxes(t.reshape((B, nqb, blk) + t.shape[2:]), 0, 1)

    o_slc = lax.map(sel_block, (to_blocks(q_rot), to_blocks(sel), to_blocks(sel_valid), pos.reshape(nqb, blk)))
    o_slc = jnp.swapaxes(o_slc, 0, 1).reshape(B, S, H, D)
    o_win = banded_attention(q_rot.reshape(B, S, 1, H, D), k_win[:, :, None], v_win[:, :, None],
                             NSA_WINDOW).reshape(B, S, H, D)
    g = jax.nn.sigmoid(gate_logits.reshape(B, S, H, 3))
    o = g[..., 0:1] * o_cmp + g[..., 1:2] * o_slc + g[..., 2:3] * o_win
    return o.reshape(B, S, H * D)


def retention_mixer(q, k, v, g_in, gn_g, pos):
    B, S, H, Dk = q.shape
    Dv = v.shape[-1]
    dt = q.dtype
    C = RET_CHUNK
    n = S // C
    log_g = jnp.log(1.0 - jnp.power(2.0, -5.0 - jnp.arange(H, dtype=jnp.float32)))
    q = rope(q, pos)
    k = rope(k, pos) * (Dk ** -0.5)
    qc = q.reshape(B, n, C, H, Dk)
    kc = k.reshape(B, n, C, H, Dk)
    vc = v.reshape(B, n, C, H, Dv)
    i = jnp.arange(C, dtype=jnp.float32)
    diff = i[:, None] - i[None, :]
    decay_mask = jnp.where(diff >= 0, jnp.exp(log_g[:, None, None] * jnp.maximum(diff, 0.0)), 0.0).astype(dt)
    zeta = jnp.exp(log_g[:, None] * (C - 1 - i)[None, :]).astype(dt)
    xi = jnp.exp(log_g[:, None] * (i + 1.0)[None, :]).astype(dt)
    g_chunk = jnp.exp(log_g * C).astype(dt)
    inner = jnp.einsum('bnchd,bnmhd->bnhcm', qc, kc) * decay_mask[None, None]
    o_inner = jnp.einsum('bnhcm,bnmhe->bnche', inner, vc)
    kv = jnp.einsum('bnmhd,hm,bnmhe->bnhde', kc, zeta, vc)

    def step(state, kv_n):
        return state * g_chunk[None, :, None, None] + kv_n, state

    _, r_prev = lax.scan(step, jnp.zeros((B, H, Dk, Dv), dt), jnp.swapaxes(kv, 0, 1))
    r_prev = jnp.swapaxes(r_prev, 0, 1)
    o_cross = jnp.einsum('bnchd,hc,bnhde->bnche', qc, xi, r_prev)
    o = (o_inner + o_cross).reshape(B, S, H, Dv).astype(jnp.float32)
    mu = jnp.mean(o, axis=-1, keepdims=True)
    var = jnp.mean(jnp.square(o - mu), axis=-1, keepdims=True)
    o = ((o - mu) * lax.rsqrt(var + GN_EPS)).reshape(B, S, H * Dv) * gn_g.astype(jnp.float32)
    return jax.nn.silu(g_in) * o.astype(dt)


def rwkv7_mixer(p, mu, w0, w2, a0, a2, g2, k_k, k_a, r_k, ln_g, ln_b):
    B, S, _ = p.shape
    H, N = RWKV_HEADS, HEAD_DIM
    f32 = lambda t: t.astype(jnp.float32)
    prev = jnp.pad(p, ((0, 0), (1, 0), (0, 0)))[:, :-1]
    xm = f32(p + mu * (prev - p))
    splits = [int(s) for s in np.cumsum([RWKV_WIDTH, RWKV_WIDTH, RWKV_WIDTH, RWKV_W_LORA, RWKV_A_LORA])]
    r, k, v, wl, al, gl = jnp.split(xm, splits, axis=-1)
    w = -jax.nn.softplus(-(f32(w0) + jnp.tanh(wl) @ f32(w2))) - 0.5
    decay = jnp.exp(-jnp.exp(w))
    a = jax.nn.sigmoid(f32(a0) + al @ f32(a2))
    g = jax.nn.sigmoid(gl) @ f32(g2)
    kk = (k * f32(k_k)).reshape(B, S, H, N)
    kk = kk / jnp.maximum(jnp.sqrt(jnp.sum(kk * kk, axis=-1, keepdims=True)), 1e-12)
    k = k * (1.0 + (a - 1.0) * f32(k_a))
    r, k, v, a, decay = [t.reshape(B, S, H, N) for t in (r, k, v, a, decay)]

    def step(state, inp):
        r_t, k_t, v_t, kk_t, a_t, w_t = inp
        sa = jnp.einsum('bhvk,bhk->bhv', state, -kk_t)
        state = (state * w_t[:, :, None, :] + sa[..., None] * (kk_t * a_t)[:, :, None, :]
                 + v_t[..., None] * k_t[:, :, None, :])
        return state, jnp.einsum('bhvk,bhk->bhv', state, r_t)

    xs = tuple(jnp.swapaxes(t, 0, 1) for t in (r, k, v, kk, a, decay))
    _, y = lax.scan(step, jnp.zeros((B, H, N, N), jnp.float32), xs)
    y = jnp.swapaxes(y, 0, 1)
    y_mu = jnp.mean(y, axis=-1, keepdims=True)
    y_var = jnp.mean(jnp.square(y - y_mu), axis=-1, keepdims=True)
    yn = ((y - y_mu) * lax.rsqrt(y_var + RWKV_LN_EPS)).reshape(B, S, H * N) * f32(ln_g) + f32(ln_b)
    bonus = (jnp.sum(r * k * f32(r_k), axis=-1, keepdims=True) * v).reshape(B, S, H * N)
    return ((yn + bonus) * g).astype(p.dtype)


def swa_sink_mixer(q, k, v, sinks, pos):
    B, S, Hq, D = q.shape
    Hkv = k.shape[2]
    G = Hq // Hkv
    q = rope(q, pos).reshape(B, S, Hkv, G, D)
    k = rope(k, pos)
    return banded_attention(q, k, v, SWA_WINDOW, sinks=sinks.reshape(Hkv, G))


def setup_inputs(seed: int = 0) -> dict:
    key = jax.random.key(seed)
    keys = iter(jax.random.split(key, 48))
    L = DEPTH
    _, n_cols = _column_layout()

    def dense(shape, fan_in):
        return jax.random.normal(next(keys), shape, jnp.float32) * (fan_in ** -0.5)

    def gain(shape):
        return 1.0 + 0.1 * jax.random.normal(next(keys), shape, jnp.float32)

    def small(shape, scale):
        return scale * jax.random.normal(next(keys), shape, jnp.float32)

    return {
        'x': jax.random.normal(next(keys), (BATCH, SEQ, D_MODEL), jnp.float32),
        'ffn1_pre_g': gain((L, D_MODEL)),
        'ffn1_post_g': gain((L, D_MODEL)),
        'ffn1_w_gate': dense((L, D_MODEL, D_FF), D_MODEL),
        'ffn1_w_up': dense((L, D_MODEL, D_FF), D_MODEL),
        'ffn1_w_down': dense((L, D_FF, D_MODEL), D_FF),
        'mix_pre_g': gain((L, D_MODEL)),
        'mix_post_g': gain((L, D_MODEL)),
        'w_in': dense((L, D_MODEL, n_cols), D_MODEL),
        'nsa_cmp_pos_k': small((L, NSA_CMP_LEN, NSA_KV_DIM), 0.5),
        'nsa_cmp_pos_v': small((L, NSA_CMP_LEN, NSA_KV_DIM), 0.5),
        'nsa_cmp_k_w1': dense((L, NSA_CMP_LEN * NSA_KV_DIM, NSA_CMP_HIDDEN), NSA_CMP_LEN * NSA_KV_DIM),
        'nsa_cmp_k_w2': dense((L, NSA_CMP_HIDDEN, NSA_KV_DIM), NSA_CMP_HIDDEN),
        'nsa_cmp_v_w1': dense((L, NSA_CMP_LEN * NSA_KV_DIM, NSA_CMP_HIDDEN), NSA_CMP_LEN * NSA_KV_DIM),
        'nsa_cmp_v_w2': dense((L, NSA_CMP_HIDDEN, NSA_KV_DIM), NSA_CMP_HIDDEN),
        'ret_gn_g': gain((L, RET_WIDTH)),
        'rwkv_mu': jax.random.uniform(next(keys), (L, RWKV_MIX_WIDTH), jnp.float32, 0.0, 1.0),
        'rwkv_w0': jax.random.uniform(next(keys), (L, RWKV_WIDTH), jnp.float32, -6.0, -1.0),
        'rwkv_w2': dense((L, RWKV_W_LORA, RWKV_WIDTH), RWKV_W_LORA),
        'rwkv_a0': small((L, RWKV_WIDTH), 0.1),
        'rwkv_a2': dense((L, RWKV_A_LORA, RWKV_WIDTH), RWKV_A_LORA),
        'rwkv_g2': dense((L, RWKV_G_LORA, RWKV_WIDTH), RWKV_G_LORA),
        'rwkv_k_k': 0.85 + small((L, RWKV_WIDTH), 0.1),
        'rwkv_k_a': gain((L, RWKV_WIDTH)),
        'rwkv_r_k': small((L, RWKV_HEADS, HEAD_DIM), 0.3),
        'rwkv_ln_g': gain((L, RWKV_WIDTH)),
        'rwkv_ln_b': small((L, RWKV_WIDTH), 0.02),
        'swa_sinks': small((L, SWA_HEADS), 1.0),
        'w_br_nsa': dense((L, NSA_WIDTH, D_MODEL), NSA_WIDTH),
        'w_br_ret': dense((L, RET_WIDTH, D_MODEL), RET_WIDTH),
        'w_br_rwkv': dense((L, RWKV_WIDTH, D_MODEL), RWKV_WIDTH),
        'w_br_swa': dense((L, SWA_WIDTH, D_MODEL), SWA_WIDTH),
        'w_out': dense((L, D_MODEL, D_MODEL), D_MODEL),
        'ffn2_pre_g': gain((L, D_MODEL)),
        'ffn2_post_g': gain((L, D_MODEL)),
        'ffn2_w_gate': dense((L, D_MODEL, D_FF), D_MODEL),
        'ffn2_w_up': dense((L, D_MODEL, D_FF), D_MODEL),
        'ffn2_w_down': dense((L, D_FF, D_MODEL), D_FF),
    }


def reference(x, ffn1_pre_g, ffn1_post_g, ffn1_w_gate, ffn1_w_up, ffn1_w_down,
              mix_pre_g, mix_post_g, w_in,
              nsa_cmp_pos_k, nsa_cmp_pos_v, nsa_cmp_k_w1, nsa_cmp_k_w2, nsa_cmp_v_w1, nsa_cmp_v_w2,
              ret_gn_g,
              rwkv_mu, rwkv_w0, rwkv_w2, rwkv_a0, rwkv_a2, rwkv_g2, rwkv_k_k, rwkv_k_a, rwkv_r_k,
              rwkv_ln_g, rwkv_ln_b,
              swa_sinks,
              w_br_nsa, w_br_ret, w_br_rwkv, w_br_swa, w_out,
              ffn2_pre_g, ffn2_post_g, ffn2_w_gate, ffn2_w_up, ffn2_w_down):
    B, S, D = x.shape
    pos = jnp.arange(S, dtype=jnp.int32)
    layout, _ = _column_layout()
    for l in range(DEPTH):
        f = swiglu(rms_norm(x, ffn1_pre_g[l]), ffn1_w_gate[l], ffn1_w_up[l], ffn1_w_down[l])
        x = x + 0.5 * rms_norm(f, ffn1_post_g[l])
        h = rms_norm(x, mix_pre_g[l])
        p = h @ w_in[l]
        c = lambda name: _cols(p, layout, name)
        y_nsa = nsa_mixer(
            c('nsa_q').reshape(B, S, NSA_HEADS, HEAD_DIM), c('nsa_k_cmp'), c('nsa_v_cmp'),
            c('nsa_k_slc'), c('nsa_v_slc'), c('nsa_k_win'), c('nsa_v_win'), c('nsa_gate'),
            nsa_cmp_pos_k[l], nsa_cmp_pos_v[l], nsa_cmp_k_w1[l], nsa_cmp_k_w2[l],
            nsa_cmp_v_w1[l], nsa_cmp_v_w2[l], pos)
        y_ret = retention_mixer(
            c('ret_q').reshape(B, S, RET_HEADS, RET_DK), c('ret_k').reshape(B, S, RET_HEADS, RET_DK),
            c('ret_v').reshape(B, S, RET_HEADS, RET_DV), c('ret_g'), ret_gn_g[l], pos)
        y_rwkv = rwkv7_mixer(c('rwkv'), rwkv_mu[l], rwkv_w0[l], rwkv_w2[l], rwkv_a0[l], rwkv_a2[l],
                             rwkv_g2[l], rwkv_k_k[l], rwkv_k_a[l], rwkv_r_k[l], rwkv_ln_g[l], rwkv_ln_b[l])
        y_swa = swa_sink_mixer(
            c('swa_q').reshape(B, S, SWA_HEADS, HEAD_DIM), c('swa_k').reshape(B, S, SWA_KV_HEADS, HEAD_DIM),
            c('swa_v').reshape(B, S, SWA_KV_HEADS, HEAD_DIM), swa_sinks[l], pos)
        gates = jax.nn.sigmoid(c('branch_gate').reshape(B, S, N_BRANCH, D))
        merged = (gates[:, :, 0] * (y_nsa @ w_br_nsa[l]) + gates[:, :, 1] * (y_ret @ w_br_ret[l])
                  + gates[:, :, 2] * (y_rwkv @ w_br_rwkv[l]) + gates[:, :, 3] * (y_swa @ w_br_swa[l]))
        x = x + rms_norm(merged @ w_out[l], mix_post_g[l])
        f = swiglu(rms_norm(x, ffn2_pre_g[l]), ffn2_w_gate[l], ffn2_w_up[l], ffn2_w_down[l])
        x = x + 0.5 * rms_norm(f, ffn2_post_g[l])
    return x
```

```python
import functools
import math

import numpy as np
import jax
import jax.numpy as jnp
from jax import lax
from jax.experimental import pallas as pl
from jax.experimental.pallas import tpu as pltpu

F32 = jnp.float32
BF16 = jnp.bfloat16
HIGHEST = lax.Precision.HIGHEST

D_MODEL = 1024
HEAD_DIM = 64
HP = 128
ROPE_THETA = 10000.0
RMS_EPS = 1e-6
GN_EPS = 1e-5
RWKV_LN_EPS = 64e-5
D_FF = 2816
NEG_INF = -1e30
FORCED_SCORE = 1e9

NSA_HEADS = 4
NSA_CMP_LEN = 32
NSA_CMP_STRIDE = 16
NSA_CMP_HIDDEN = 256
NSA_SEL_BLOCK = 64
NSA_TOP_N = 16
NSA_WINDOW = 512
RET_HEADS = 4
RET_DV = 128
RET_CHUNK = 128
RWKV_HEADS = 4
RWKV_CHUNK = 64
SWA_HEADS = 4
SWA_KV_HEADS = 2
SWA_WINDOW = 128
ATT_BLOCK = 128

VMEM_LIMIT = 56 * 1024 * 1024


def _cp(*sem):
    return pltpu.CompilerParams(dimension_semantics=sem, vmem_limit_bytes=VMEM_LIMIT)


def _const_spec(shape):
    nd = len(shape)
    return pl.BlockSpec(shape, lambda *_: (0,) * nd, pipeline_mode=pl.Buffered(1))


def _dot(a, b):
    return jnp.dot(a.astype(BF16), b.astype(BF16), preferred_element_type=F32)


def _dot_nt(a, b):
    return lax.dot_general(a.astype(BF16), b.astype(BF16), (((1,), (1,)), ((), ())),
                           preferred_element_type=F32)


def _dot_tn(a, b):
    return lax.dot_general(a.astype(BF16), b.astype(BF16), (((0,), (0,)), ((), ())),
                           preferred_element_type=F32)


def _dot_hi(a, b):
    return jnp.dot(a, b, precision=HIGHEST, preferred_element_type=F32)


def _rms(x, g):
    return x * lax.rsqrt(jnp.mean(x * x, axis=-1, keepdims=True) + RMS_EPS) * g


def _rope(t, cos, sin_lo, sin_hi):
    return t * cos + pltpu.roll(t, 32, 1) * sin_hi - pltpu.roll(t, HP - 32, 1) * sin_lo


def _ffn_kernel(x_ref, pre_ref, post_ref, wg_ref, wu_ref, wd_ref, o_ref, *, ck):
    x = x_ref[...]
    h = _rms(x, pre_ref[...]).astype(BF16)
    acc = jnp.zeros(x.shape, F32)
    for c in range(D_FF // ck):
        g = jnp.dot(h, wg_ref[:, c * ck:(c + 1) * ck], preferred_element_type=F32)
        u = jnp.dot(h, wu_ref[:, c * ck:(c + 1) * ck], preferred_element_type=F32)
        a = (jax.nn.silu(g) * u).astype(BF16)
        acc = acc + jnp.dot(a, wd_ref[c * ck:(c + 1) * ck, :], preferred_element_type=F32)
    o_ref[...] = x + 0.5 * _rms(acc, post_ref[...])


def _ffn(x, pre_g, post_g, wg, wu, wd, tm=512):
    T, D = x.shape
    return pl.pallas_call(
        functools.partial(_ffn_kernel, ck=256),
        grid=(T // tm,),
        in_specs=[pl.BlockSpec((tm, D), lambda i: (i, 0)),
                  _const_spec((1, D)), _const_spec((1, D)),
                  _const_spec((D, D_FF)), _const_spec((D, D_FF)), _const_spec((D_FF, D))],
        out_specs=pl.BlockSpec((tm, D), lambda i: (i, 0)),
        out_shape=jax.ShapeDtypeStruct((T, D), F32),
        compiler_params=_cp("arbitrary"),
    )(x, pre_g, post_g, wg, wu, wd)


NSA_COLS = 4 * HP + 4 * HP + HP
RET_COLS = 4 * HP + 4 * HP + 512 + 512
RW_COLS = 3 * 4 * HP + 3 * HP
SWA_COLS = 4 * HP + 2 * HP + 2 * HP
RW_OUT = 7 * 4 * HP


def _inproj_kernel(x_ref, pre_ref, rope_ref, wn_ref, wr_ref, ww_ref, ws_ref,
                   mu_ref, w0_ref, w2_ref, a0_ref, a2_ref, g2_ref, kk_ref, ka_ref,
                   nq_ref, nqr_ref, nslc_ref, nwin_ref, ncmp_ref,
                   rq_ref, rk_ref, rv_ref, rg_ref, rw_ref, sq_ref, skv_ref,
                   carry_ref, *, tiles_per_seq):
    i = pl.program_id(0)
    x = x_ref[...]
    tm = x.shape[0]
    h = _rms(x, pre_ref[...]).astype(BF16)
    cos = rope_ref[:, 0:HP]
    s_lo = rope_ref[:, HP:2 * HP]
    s_hi = rope_ref[:, 2 * HP:3 * HP]
    rope = lambda t: _rope(t, cos, s_lo, s_hi)
    hs = lambda t, j: t[:, j * HP:(j + 1) * HP]

    p = jnp.dot(h, wn_ref[...], preferred_element_type=F32)
    nq_ref[...] = p[:, 0:4 * HP]
    for j in range(4):
        nqr_ref[:, j * HP:(j + 1) * HP] = rope(hs(p, j))
    nslc_ref[:, 0:HP] = rope(hs(p, 4))
    nslc_ref[:, HP:2 * HP] = hs(p, 5)
    nwin_ref[:, 0:HP] = rope(hs(p, 6))
    nwin_ref[:, HP:2 * HP] = hs(p, 7)
    ncmp_ref[...] = hs(p, 8)

    p = jnp.dot(h, wr_ref[...], preferred_element_type=F32)
    for j in range(4):
        rq_ref[:, j * HP:(j + 1) * HP] = rope(hs(p, j))
        rk_ref[:, j * HP:(j + 1) * HP] = rope(hs(p, 4 + j))
    rv_ref[...] = p[:, 8 * HP:8 * HP + 512]
    rg_ref[...] = p[:, 8 * HP + 512:8 * HP + 1024]

    p = jnp.dot(h, ws_ref[...], preferred_element_type=F32)
    for j in range(4):
        sq_ref[:, j * HP:(j + 1) * HP] = rope(hs(p, j))
    for j in range(2):
        skv_ref[:, j * HP:(j + 1) * HP] = rope(hs(p, 4 + j))
    skv_ref[:, 2 * HP:4 * HP] = p[:, 6 * HP:8 * HP]

    p = jnp.dot(h, ww_ref[...], preferred_element_type=F32)
    @pl.when(i == 0)
    def _():
        carry_ref[...] = jnp.zeros(carry_ref.shape, F32)

    first = jnp.where(i % tiles_per_seq == 0, 0.0, carry_ref[0:1, :])
    row = lax.broadcasted_iota(jnp.int32, p.shape, 0)
    prev = jnp.where(row == 0, first, pltpu.roll(p, 1, 0))
    carry_ref[0:1, :] = p[tm - 1:tm, :]
    xm = p + mu_ref[...] * (prev - p)
    W4 = 4 * HP
    r = xm[:, 0:W4]
    k = xm[:, W4:2 * W4]
    v = xm[:, 2 * W4:3 * W4]
    wl = xm[:, 3 * W4:3 * W4 + HP]
    al = xm[:, 3 * W4 + HP:3 * W4 + 2 * HP]
    gl = xm[:, 3 * W4 + 2 * HP:3 * W4 + 3 * HP]
    z = -(w0_ref[...] + _dot(jnp.tanh(wl), w2_ref[...]))
    softplus = jnp.maximum(z, 0.0) + jnp.log(1.0 + jnp.exp(-jnp.abs(z)))
    w = -softplus - 0.5
    lane = lax.broadcasted_iota(jnp.int32, w.shape, 1)
    logw = jnp.where(lane % HP < HEAD_DIM, -jnp.exp(w), 0.0)
    a = jax.nn.sigmoid(a0_ref[...] + _dot(al, a2_ref[...]))
    g = _dot(jax.nn.sigmoid(gl), g2_ref[...])
    kkr = k * kk_ref[...]
    k2 = k * (1.0 + (a - 1.0) * ka_ref[...])
    rw_ref[:, 0:W4] = r
    rw_ref[:, W4:2 * W4] = k2
    rw_ref[:, 2 * W4:3 * W4] = v
    for j in range(4):
        kj = hs(kkr, j)
        nrm = jnp.maximum(jnp.sqrt(jnp.sum(kj * kj, axis=-1, keepdims=True)), 1e-12)
        kkj = kj / nrm
        rw_ref[:, 3 * W4 + j * HP:3 * W4 + (j + 1) * HP] = -kkj
        rw_ref[:, 4 * W4 + j * HP:4 * W4 + (j + 1) * HP] = kkj * hs(a, j)
    rw_ref[:, 5 * W4:6 * W4] = logw
    rw_ref[:, 6 * W4:7 * W4] = g


def _inproj(x, S, pre_g, rope_tab, wn, wr, ww, ws, mu, w0, w2, a0, a2, g2, k_k, k_a, tm=256):
    T, D = x.shape
    row = lambda w: pl.BlockSpec((tm, w), lambda i: (i, 0))
    tps = S // tm
    outs = [4 * HP, 4 * HP, 2 * HP, 2 * HP, HP, 4 * HP, 4 * HP, 512, 512, RW_OUT, 4 * HP, 4 * HP]
    return pl.pallas_call(
        functools.partial(_inproj_kernel, tiles_per_seq=tps),
        grid=(T // tm,),
        in_specs=[row(D), _const_spec((1, D)),
                  pl.BlockSpec((tm, 3 * HP), lambda i: (i % tps, 0)),
                  _const_spec(wn.shape), _const_spec(wr.shape), _const_spec(ww.shape), _const_spec(ws.shape),
                  _const_spec(mu.shape), _const_spec(w0.shape), _const_spec(w2.shape), _const_spec(a0.shape),
                  _const_spec(a2.shape), _const_spec(g2.shape), _const_spec(k_k.shape), _const_spec(k_a.shape)],
        out_specs=[row(w) for w in outs],
        out_shape=[jax.ShapeDtypeStruct((T, w), F32) for w in outs],
        scratch_shapes=[pltpu.VMEM((8, RW_COLS), F32)],
        compiler_params=_cp("arbitrary"),
    )(x, pre_g, rope_tab, wn, wr, ww, ws, mu, w0, w2, a0, a2, g2, k_k, k_a)


def _compress_kernel(h_ref, add_ref, w1_ref, w2_ref, kc_ref, vc_ref):
    hb = h_ref[...]
    n_half = hb.shape[0]
    for t, o_ref in ((0, kc_ref), (1, vc_ref)):
        top = _dot(hb + add_ref[2 * t:2 * t + 1, :], w1_ref[2 * t])
        bot = _dot(hb + add_ref[2 * t + 1:2 * t + 2, :], w1_ref[2 * t + 1])
        pre = top + pltpu.roll(bot, n_half - 1, 0)
        o_ref[...] = _dot(jax.nn.gelu(pre), w2_ref[t])


def _compress(hb, add, w1, w2):
    B, n_half, W = hb.shape
    return pl.pallas_call(
        _compress_kernel,
        grid=(B,),
        in_specs=[pl.BlockSpec((None, n_half, W), lambda b: (b, 0, 0)),
                  _const_spec(add.shape), _const_spec(w1.shape), _const_spec(w2.shape)],
        out_specs=[pl.BlockSpec((None, n_half, HP), lambda b: (b, 0, 0))] * 2,
        out_shape=[jax.ShapeDtypeStruct((B, n_half, HP), F32)] * 2,
        compiler_params=_cp("arbitrary"),
    )(hb, add, w1, w2)


def _cmp_kernel(q_ref, kc_ref, vc_ref, ovl_ref, o_ref, sel_ref, *, n_top):
    i = pl.program_id(1)
    tq = q_ref.shape[0]
    n_half = kc_ref.shape[0]
    n_sel = ovl_ref.shape[0]
    kc = kc_ref[...]
    vc = vc_ref[...]
    tpos = i * tq + lax.broadcasted_iota(jnp.int32, (tq, n_half), 0)
    n_id = lax.broadcasted_iota(jnp.int32, (tq, n_half), 1)
    mask = (n_id * NSA_CMP_STRIDE + NSA_CMP_LEN - 1 <= tpos) & (n_id < n_half - 1)
    maskf = mask.astype(F32)
    imp_t = jnp.zeros((n_sel, tq), F32)
    for hd in range(NSA_HEADS):
        q = q_ref[:, hd * HP:(hd + 1) * HP]
        s = jnp.where(mask, _dot_nt(q, kc), NEG_INF)
        m = jnp.max(s, axis=-1, keepdims=True)
        p = jnp.exp(s - m) * maskf
        p = p / jnp.maximum(jnp.sum(p, axis=-1, keepdims=True), 1e-30)
        o_ref[:, hd * HP:(hd + 1) * HP] = _dot(p, vc)
        imp_t = imp_t + _dot_nt(ovl_ref[...], p)
    blk = lax.broadcasted_iota(jnp.int32, (n_sel, tq), 0)
    cur = (i * tq + lax.broadcasted_iota(jnp.int32, (n_sel, tq), 1)) // NSA_SEL_BLOCK
    forced = (blk == 0) | (blk == cur) | (blk == cur - 1)
    valid = blk <= cur
    score = jnp.where(forced, FORCED_SCORE, jnp.where(valid, imp_t, -FORCED_SCORE))
    rank = jnp.zeros((n_sel, tq), F32)
    for b in range(n_sel):
        sb = score[b:b + 1, :]
        before = (sb > score) | ((sb == score) & (blk > b))
        rank = rank + before.astype(F32)
    sel_t = ((rank < n_top) & valid).astype(F32)
    sel_ref[...] = sel_t.T


def _cmp_attn(q, kc, vc, ovl_t, n_top, tq=ATT_BLOCK):
    B, S, W = q.shape
    n_half = kc.shape[1]
    n_sel = ovl_t.shape[0]
    return pl.pallas_call(
        functools.partial(_cmp_kernel, n_top=n_top),
        grid=(B, S // tq),
        in_specs=[pl.BlockSpec((None, tq, W), lambda b, i: (b, i, 0)),
                  pl.BlockSpec((None, n_half, HP), lambda b, i: (b, 0, 0)),
                  pl.BlockSpec((None, n_half, HP), lambda b, i: (b, 0, 0)),
                  _const_spec(ovl_t.shape)],
        out_specs=[pl.BlockSpec((None, tq, W), lambda b, i: (b, i, 0)),
                   pl.BlockSpec((None, tq, n_sel), lambda b, i: (b, i, 0))],
        out_shape=[jax.ShapeDtypeStruct((B, S, W), F32), jax.ShapeDtypeStruct((B, S, n_sel), F32)],
        compiler_params=_cp("arbitrary", "arbitrary"),
    )(q, kc, vc, ovl_t)


def _flash_kernel(*refs, n_kv, group, window, use_sel, use_sink):
    it = iter(refs)
    q_ref = next(it)
    kv_ref = next(it)
    sel_ref = next(it) if use_sel else None
    exp_ref = next(it) if use_sel else None
    sink_ref = next(it) if use_sink else None
    o_ref = next(it)
    i = pl.program_id(1)
    tq = q_ref.shape[0]
    tk = tq
    rows = group * tq
    if window is None:
        j_lo = 0
    else:
        j_lo = jnp.maximum(i - (-(-(window - 1) // tk)), 0)
    tpos = i * tq + lax.broadcasted_iota(jnp.int32, (rows, tk), 0) % tq
    kcol = lax.broadcasted_iota(jnp.int32, (rows, tk), 1)
    if use_sel:
        selm = sel_ref[...].astype(BF16)
    for kh in range(n_kv):
        qs = jnp.concatenate([q_ref[:, (kh * group + g) * HP:(kh * group + g + 1) * HP]
                              for g in range(group)], axis=0).astype(BF16)

        def body(j, carry):
            m, l, acc = carry
            r0 = pl.multiple_of(j * tk, tk)
            k_t = kv_ref[pl.ds(r0, tk), kh * HP:(kh + 1) * HP]
            v_t = kv_ref[pl.ds(r0, tk), (n_kv + kh) * HP:(n_kv + kh + 1) * HP]
            s = _dot_nt(qs, k_t)
            diff = tpos - (j * tk + kcol)
            mask = diff >= 0
            if window is not None:
                mask = mask & (diff < window)
            if use_sel:
                ms = jnp.dot(selm, exp_ref[j], preferred_element_type=F32)
                mask = mask & (jnp.concatenate([ms] * group, axis=0) > 0.5)
            s = jnp.where(mask, s, NEG_INF)
            m_new = jnp.maximum(m, jnp.max(s, axis=-1, keepdims=True))
            alpha = jnp.exp(m - m_new)
            p = jnp.exp(s - m_new) * mask.astype(F32)
            l = l * alpha + jnp.sum(p, axis=-1, keepdims=True)
            acc = acc * alpha + _dot(p, v_t)
            return m_new, l, acc

        init = (jnp.full((rows, 1), NEG_INF, F32), jnp.zeros((rows, 1), F32), jnp.zeros((rows, HP), F32))
        m, l, acc = lax.fori_loop(j_lo, i + 1, body, init)
        if use_sink:
            sk = jnp.concatenate([jnp.full((tq, 1), sink_ref[kh * group + g], F32) for g in range(group)], axis=0)
            m_f = jnp.maximum(m, sk)
            scale = jnp.exp(m - m_f)
            l = l * scale + jnp.exp(sk - m_f)
            acc = acc * scale
        o = acc / l
        for g in range(group):
            o_ref[:, (kh * group + g) * HP:(kh * group + g + 1) * HP] = o[g * tq:(g + 1) * tq, :]


def _flash(q, kv, n_kv, group, window=None, sel=None, expand=None, sinks=None, tq=ATT_BLOCK):
    B, S, W = q.shape
    use_sel = sel is not None
    use_sink = sinks is not None
    args = [q, kv]
    in_specs = [pl.BlockSpec((None, tq, W), lambda b, i: (b, i, 0)),
                pl.BlockSpec((None, S, kv.shape[2]), lambda b, i: (b, 0, 0))]
    if use_sel:
        args += [sel, expand]
        in_specs += [pl.BlockSpec((None, tq, sel.shape[2]), lambda b, i: (b, i, 0)), _const_spec(expand.shape)]
    if use_sink:
        args.append(sinks)
        in_specs.append(pl.BlockSpec(memory_space=pltpu.SMEM))
    return pl.pallas_call(
        functools.partial(_flash_kernel, n_kv=n_kv, group=group, window=window, use_sel=use_sel, use_sink=use_sink),
        grid=(B, S // tq),
        in_specs=in_specs,
        out_specs=pl.BlockSpec((None, tq, W), lambda b, i: (b, i, 0)),
        out_shape=jax.ShapeDtypeStruct((B, S, W), F32),
        compiler_params=_cp("arbitrary", "arbitrary"),
    )(*args)


def _ret_kernel(q_ref, k_ref, v_ref, g_ref, dm_ref, zeta_ref, xi_ref, gch_ref, gn_ref, o_ref, st_ref):
    c = pl.program_id(1)

    @pl.when(c == 0)
    def _():
        st_ref[...] = jnp.zeros(st_ref.shape, F32)

    for hd in range(RET_HEADS):
        q = q_ref[:, hd * HP:(hd + 1) * HP]
        k = k_ref[:, hd * HP:(hd + 1) * HP]
        v = v_ref[:, hd * RET_DV:(hd + 1) * RET_DV]
        inner = _dot_nt(q, k) * dm_ref[hd]
        o = _dot(inner, v)
        state = st_ref[hd]
        o = o + _dot(q * xi_ref[hd], state)
        st_ref[hd] = state * gch_ref[hd] + _dot_tn(k * zeta_ref[hd], v)
        mu = jnp.mean(o, axis=-1, keepdims=True)
        d = o - mu
        var = jnp.mean(d * d, axis=-1, keepdims=True)
        on = d * lax.rsqrt(var + GN_EPS) * gn_ref[:, hd * RET_DV:(hd + 1) * RET_DV]
        o_ref[:, hd * RET_DV:(hd + 1) * RET_DV] = jax.nn.silu(g_ref[:, hd * RET_DV:(hd + 1) * RET_DV]) * on


def _retention(q, k, v, g, dmask, zeta, xi, gch, gn_g):
    B, S, _ = q.shape
    C = RET_CHUNK
    blk = lambda w: pl.BlockSpec((None, C, w), lambda b, c: (b, c, 0))
    return pl.pallas_call(
        _ret_kernel,
        grid=(B, S // C),
        in_specs=[blk(4 * HP), blk(4 * HP), blk(512), blk(512),
                  _const_spec(dmask.shape), _const_spec(zeta.shape), _const_spec(xi.shape),
                  _const_spec(gch.shape), _const_spec(gn_g.shape)],
        out_specs=blk(512),
        out_shape=jax.ShapeDtypeStruct((B, S, 512), F32),
        scratch_shapes=[pltpu.VMEM((RET_HEADS, HP, RET_DV), F32)],
        compiler_params=_cp("arbitrary", "arbitrary"),
    )(q, k, v, g, dmask, zeta, xi, gch, gn_g)


def _rwkv_kernel(rw_ref, rk_ref, lng_ref, lnb_ref, o_ref, z_ref):
    c = pl.program_id(1)

    @pl.when(c == 0)
    def _():
        z_ref[...] = jnp.zeros(z_ref.shape, F32)

    L = rw_ref.shape[0]
    W4 = 4 * HP
    ri = lax.broadcasted_iota(jnp.int32, (L, L), 0)
    ci = lax.broadcasted_iota(jnp.int32, (L, L), 1)
    incl = ri >= ci
    strict = ri > ci
    tri = incl.astype(F32)
    dr = lax.broadcasted_iota(jnp.int32, (HP, HP), 0)
    dc = lax.broadcasted_iota(jnp.int32, (HP, HP), 1)
    lane = lax.broadcasted_iota(jnp.int32, (L, HP), 1)
    real = (lane < HEAD_DIM).astype(F32)
    n_sq = int(math.log2(L))
    for hd in range(RWKV_HEADS):
        col = lambda gi: rw_ref[:, gi * W4 + hd * HP:gi * W4 + (hd + 1) * HP]
        r, k, v, a, b, lw, g = (col(gi) for gi in range(7))
        cum = _dot_hi(tri, lw)
        cum_last = cum[L - 1:L, :]
        e_cum = jnp.exp(cum)
        e_inv = jnp.exp(-cum)
        at = a * jnp.exp(cum - lw)
        rt = r * e_cum
        bt = b * e_inv
        kt = k * e_inv
        e_rem = jnp.exp(cum_last - cum)
        bh = b * e_rem
        kh = k * e_rem
        a_ab = jnp.where(strict, _dot_nt(at, bt), 0.0)
        a_ak = jnp.where(strict, _dot_nt(at, kt), 0.0)
        a_rb = jnp.where(incl, _dot_nt(rt, bt), 0.0)
        a_rk = jnp.where(incl, _dot_nt(rt, kt), 0.0)
        x1 = at
        x2 = _dot(a_ak, v)
        npow = a_ab
        for s in range(n_sq):
            x1 = x1 + _dot_hi(npow, x1)
            x2 = x2 + _dot_hi(npow, x2)
            if s + 1 < n_sq:
                npow = _dot_hi(npow, npow)
        z = z_ref[hd]
        y = _dot(a_rb, x2) + _dot(a_rk, v) + _dot(rt + _dot(a_rb, x1), z)
        m_mat = jnp.where(dr == dc, jnp.exp(cum_last), 0.0) + _dot_tn(bh, x1)
        z_ref[hd] = _dot_hi(m_mat, z) + _dot_tn(bh, x2) + _dot_tn(kh, v)
        mu = jnp.sum(y, axis=-1, keepdims=True) * (1.0 / HEAD_DIM)
        d = (y - mu) * real
        var = jnp.sum(d * d, axis=-1, keepdims=True) * (1.0 / HEAD_DIM)
        yn = d * lax.rsqrt(var + RWKV_LN_EPS) * lng_ref[:, hd * HP:(hd + 1) * HP] + lnb_ref[:, hd * HP:(hd + 1) * HP]
        bonus = jnp.sum(r * k * rk_ref[:, hd * HP:(hd + 1) * HP], axis=-1, keepdims=True) * v
        o_ref[:, hd * HP:(hd + 1) * HP] = (yn + bonus) * g


def _rwkv(rw, r_k, ln_g, ln_b):
    B, S, _ = rw.shape
    L = RWKV_CHUNK
    return pl.pallas_call(
        _rwkv_kernel,
        grid=(B, S // L),
        in_specs=[pl.BlockSpec((None, L, RW_OUT), lambda b, c: (b, c, 0)),
                  _const_spec(r_k.shape), _const_spec(ln_g.shape), _const_spec(ln_b.shape)],
        out_specs=pl.BlockSpec((None, L, 4 * HP), lambda b, c: (b, c, 0)),
        out_shape=jax.ShapeDtypeStruct((B, S, 4 * HP), F32),
        scratch_shapes=[pltpu.VMEM((RWKV_HEADS, HP, HP), F32)],
        compiler_params=_cp("arbitrary", "arbitrary"),
    )(rw, r_k, ln_g, ln_b)


def _merge_kernel(x_ref, ocmp_ref, oslc_ref, owin_ref, yret_ref, yrw_ref, yswa_ref,
                  pre_ref, post_ref, wg_ref, wn_ref, wr_ref, ww_ref, ws_ref, wo_ref, o_ref):
    x = x_ref[...]
    D = x.shape[1]
    W4 = 4 * HP
    h = _rms(x, pre_ref[...]).astype(BF16)
    gate = lambda j, w: jax.nn.sigmoid(jnp.dot(h, wg_ref[:, j:j + w], preferred_element_type=F32))
    y_nsa = (gate(4 * D, W4) * ocmp_ref[...] + gate(4 * D + W4, W4) * oslc_ref[...]
             + gate(4 * D + 2 * W4, W4) * owin_ref[...])
    merged = gate(0, D) * _dot(y_nsa, wn_ref[...])
    merged = merged + gate(D, D) * _dot(yret_ref[...], wr_ref[...])
    merged = merged + gate(2 * D, D) * _dot(yrw_ref[...], ww_ref[...])
    merged = merged + gate(3 * D, D) * _dot(yswa_ref[...], ws_ref[...])
    o_ref[...] = x + _rms(_dot(merged, wo_ref[...]), post_ref[...])


def _merge(x, ocmp, oslc, owin, yret, yrw, yswa, pre_g, post_g, wg, wn, wr, ww, ws, wo, tm=256):
    T, D = x.shape
    row = lambda w: pl.BlockSpec((tm, w), lambda i: (i, 0))
    return pl.pallas_call(
        _merge_kernel,
        grid=(T // tm,),
        in_specs=[row(D), row(4 * HP), row(4 * HP), row(4 * HP), row(512), row(4 * HP), row(4 * HP),
                  _const_spec((1, D)), _const_spec((1, D)), _const_spec(wg.shape), _const_spec(wn.shape),
                  _const_spec(wr.shape), _const_spec(ww.shape), _const_spec(ws.shape), _const_spec(wo.shape)],
        out_specs=row(D),
        out_shape=jax.ShapeDtypeStruct((T, D), F32),
        compiler_params=_cp("arbitrary"),
    )(x, ocmp, oslc, owin, yret, yrw, yswa, pre_g, post_g, wg, wn, wr, ww, ws, wo)


def _column_layout():
    spec = (
        ('nsa_q', 256), ('nsa_k_cmp', 64), ('nsa_v_cmp', 64), ('nsa_k_slc', 64), ('nsa_v_slc', 64),
        ('nsa_k_win', 64), ('nsa_v_win', 64), ('nsa_gate', 12),
        ('ret_q', 256), ('ret_k', 256), ('ret_v', 512), ('ret_g', 512),
        ('rwkv', 1024),
        ('swa_q', 256), ('swa_k', 128), ('swa_v', 128),
        ('branch_gate', 4 * D_MODEL),
    )
    layout, start = {}, 0
    for name, width in spec:
        layout[name] = (start, start + width)
        start += width
    return layout


def _pad_heads(w, axis=-1):
    w = jnp.moveaxis(w, axis, -1)
    lead = w.shape[:-1]
    n = w.shape[-1] // HEAD_DIM
    w = w.reshape(lead + (n, HEAD_DIM))
    w = jnp.pad(w, [(0, 0)] * len(lead) + [(0, 0), (0, HP - HEAD_DIM)])
    return jnp.moveaxis(w.reshape(lead + (n * HP,)), -1, axis)


def _pad_cols(w, width):
    return jnp.pad(w, [(0, 0)] * (w.ndim - 1) + [(0, width - w.shape[-1])])


def _rope_table(S):
    half = HEAD_DIM // 2
    inv_freq = jnp.power(ROPE_THETA, -jnp.arange(half, dtype=F32) * 2.0 / HEAD_DIM)
    ang = jnp.arange(S, dtype=jnp.int32).astype(F32)[:, None] * inv_freq[None, :]
    cos, sin, z = jnp.cos(ang), jnp.sin(ang), jnp.zeros((S, half), F32)
    zp = jnp.zeros((S, HP - HEAD_DIM), F32)
    return jnp.concatenate([cos, cos, zp, sin, z, zp, z, sin, zp], axis=1)


def _retention_tables():
    H, C = RET_HEADS, RET_CHUNK
    log_g = jnp.log(1.0 - jnp.power(2.0, -5.0 - jnp.arange(H, dtype=F32)))
    i = jnp.arange(C, dtype=F32)
    diff = i[:, None] - i[None, :]
    dmask = jnp.where(diff >= 0, jnp.exp(log_g[:, None, None] * jnp.maximum(diff, 0.0)), 0.0)
    zeta = jnp.exp(log_g[:, None] * (C - 1 - i)[None, :])
    xi = jnp.exp(log_g[:, None] * (i + 1.0)[None, :])
    gch = jnp.exp(log_g * C)
    bc = lambda t: jnp.broadcast_to(t[:, :, None], (H, C, HP))
    return dmask, bc(zeta), bc(xi), jnp.broadcast_to(gch[:, None, None], (H, HP, RET_DV))


def kernel(x, ffn1_pre_g, ffn1_post_g, ffn1_w_gate, ffn1_w_up, ffn1_w_down, mix_pre_g, mix_post_g, w_in, nsa_cmp_pos_k, nsa_cmp_pos_v, nsa_cmp_k_w1, nsa_cmp_k_w2, nsa_cmp_v_w1, nsa_cmp_v_w2, ret_gn_g, rwkv_mu, rwkv_w0, rwkv_w2, rwkv_a0, rwkv_a2, rwkv_g2, rwkv_k_k, rwkv_k_a, rwkv_r_k, rwkv_ln_g, rwkv_ln_b, swa_sinks, w_br_nsa, w_br_ret, w_br_rwkv, w_br_swa, w_out, ffn2_pre_g, ffn2_post_g, ffn2_w_gate, ffn2_w_up, ffn2_w_down):
    B, S, D = x.shape
    T = B * S
    depth = w_in.shape[0]
    lay = _column_layout()
    n_half = S // NSA_CMP_STRIDE
    n_sel = S // NSA_SEL_BLOCK
    n_top = min(NSA_TOP_N, n_sel)
    scale = HEAD_DIM ** -0.5
    rope_tab = _rope_table(S)
    dmask, zeta, xi, gch = _retention_tables()
    cs = np.arange(n_half) * NSA_CMP_STRIDE
    ss = np.arange(n_sel) * NSA_SEL_BLOCK
    ovl = ((cs[None, :] <= ss[:, None] + NSA_SEL_BLOCK - 1) & (cs[None, :] + NSA_CMP_LEN - 1 >= ss[:, None])
           & (np.arange(n_half)[None, :] < n_half - 1))
    ovl_t = jnp.asarray(ovl, BF16)
    key_blk = (np.arange(S) // NSA_SEL_BLOCK).reshape(S // ATT_BLOCK, 1, ATT_BLOCK)
    expand = jnp.asarray(key_blk == np.arange(n_sel)[None, :, None], BF16)
    row = lambda t: t.reshape(1, -1)

    xf = x.reshape(T, D)
    for l in range(depth):
        wl = w_in[l]
        c = lambda name: wl[:, lay[name][0]:lay[name][1]]
        xf = _ffn(xf, row(ffn1_pre_g[l]), row(ffn1_post_g[l]), ffn1_w_gate[l].astype(BF16),
                  ffn1_w_up[l].astype(BF16), ffn1_w_down[l].astype(BF16))
        w_nsa = jnp.concatenate([_pad_heads(c('nsa_q') * scale), _pad_cols(c('nsa_k_slc'), HP),
                                 _pad_cols(c('nsa_v_slc'), HP), _pad_cols(c('nsa_k_win'), HP),
                                 _pad_cols(c('nsa_v_win'), HP), c('nsa_k_cmp'), c('nsa_v_cmp')], axis=1).astype(BF16)
        w_ret = jnp.concatenate([_pad_heads(c('ret_q')), _pad_heads(c('ret_k') * scale),
                                 c('ret_v'), c('ret_g')], axis=1).astype(BF16)
        rw0 = lay['rwkv'][0]
        rcol = lambda s, e: wl[:, rw0 + s:rw0 + e]
        w_rw = jnp.concatenate([_pad_heads(rcol(0, 256)), _pad_heads(rcol(256, 512)), _pad_heads(rcol(512, 768)),
                                _pad_cols(rcol(768, 832), HP), _pad_cols(rcol(832, 896), HP), rcol(896, 1024)],
                               axis=1).astype(BF16)
        w_swa = jnp.concatenate([_pad_heads(c('swa_q') * scale), _pad_heads(c('swa_k')),
                                 _pad_heads(c('swa_v'))], axis=1).astype(BF16)
        mu = rwkv_mu[l]
        mu_p = row(jnp.concatenate([_pad_heads(mu[0:256]), _pad_heads(mu[256:512]), _pad_heads(mu[512:768]),
                                    _pad_cols(mu[768:832], HP), _pad_cols(mu[832:896], HP), mu[896:1024]]))
        w2_p = jnp.pad(_pad_heads(rwkv_w2[l]), ((0, HP - rwkv_w2.shape[1]), (0, 0)))
        a2_p = jnp.pad(_pad_heads(rwkv_a2[l]), ((0, HP - rwkv_a2.shape[1]), (0, 0)))
        g2_p = _pad_heads(rwkv_g2[l])
        (nsa_q, nsa_qr, kv_slc, kv_win, kv_cmp, ret_q, ret_k, ret_v, ret_g, rw, swa_q, swa_kv) = _inproj(
            xf, S, row(mix_pre_g[l]), rope_tab, w_nsa, w_ret, w_rw, w_swa, mu_p,
            row(_pad_heads(rwkv_w0[l])), w2_p, row(_pad_heads(rwkv_a0[l])), a2_p, g2_p,
            row(_pad_heads(rwkv_k_k[l])), row(_pad_heads(rwkv_k_a[l])))
        b3 = lambda t: t.reshape(B, S, t.shape[-1])

        half = NSA_CMP_STRIDE * HEAD_DIM
        z16 = jnp.zeros((NSA_CMP_STRIDE, HEAD_DIM), F32)
        emb = lambda pk, pv: jnp.concatenate([pk, pv], axis=1).reshape(1, -1)
        add = jnp.concatenate([emb(nsa_cmp_pos_k[l][:16], z16), emb(nsa_cmp_pos_k[l][16:], z16),
                               emb(z16, nsa_cmp_pos_v[l][:16]), emb(z16, nsa_cmp_pos_v[l][16:])], axis=0)

        def w1_embed(w1_half, is_v):
            w = w1_half.reshape(NSA_CMP_STRIDE, HEAD_DIM, NSA_CMP_HIDDEN)
            zz = jnp.zeros_like(w)
            parts = (zz, w) if is_v else (w, zz)
            return jnp.concatenate(parts, axis=1).reshape(2 * half, NSA_CMP_HIDDEN)

        w1 = jnp.stack([w1_embed(nsa_cmp_k_w1[l][:half], False), w1_embed(nsa_cmp_k_w1[l][half:], False),
                        w1_embed(nsa_cmp_v_w1[l][:half], True), w1_embed(nsa_cmp_v_w1[l][half:], True)]).astype(BF16)
        w2 = jnp.stack([_pad_cols(nsa_cmp_k_w2[l], HP), _pad_cols(nsa_cmp_v_w2[l], HP)]).astype(BF16)
        kc, vc = _compress(kv_cmp.reshape(B, n_half, NSA_CMP_STRIDE * 2 * HEAD_DIM), add, w1, w2)
        o_cmp, sel = _cmp_attn(b3(nsa_q), kc, vc, ovl_t, n_top)
        o_slc = _flash(b3(nsa_qr), b3(kv_slc), 1, NSA_HEADS, sel=sel, expand=expand)
        o_win = _flash(b3(nsa_qr), b3(kv_win), 1, NSA_HEADS, window=NSA_WINDOW)
        y_ret = _retention(b3(ret_q), b3(ret_k), b3(ret_v), b3(ret_g), dmask, zeta, xi, gch, row(ret_gn_g[l]))
        y_rw = _rwkv(b3(rw), row(_pad_heads(rwkv_r_k[l].reshape(-1))), row(_pad_heads(rwkv_ln_g[l])),
                     row(_pad_heads(rwkv_ln_b[l])))
        y_swa = _flash(b3(swa_q), b3(swa_kv), SWA_KV_HEADS, SWA_HEADS // SWA_KV_HEADS, window=SWA_WINDOW,
                       sinks=swa_sinks[l])
        gcol = c('nsa_gate').reshape(D, NSA_HEADS, 3)
        g_exp = [jnp.repeat(gcol[:, :, j], HP, axis=1) for j in range(3)]
        w_gate = jnp.concatenate([c('branch_gate')] + g_exp, axis=1).astype(BF16)
        f2 = lambda t: t.reshape(T, t.shape[-1])
        xf = _merge(xf, f2(o_cmp), f2(o_slc), f2(o_win), f2(y_ret), f2(y_rw), f2(y_swa),
                    row(mix_pre_g[l]), row(mix_post_g[l]), w_gate,
                    _pad_heads(w_br_nsa[l], axis=0).astype(BF16), w_br_ret[l].astype(BF16),
                    _pad_heads(w_br_rwkv[l], axis=0).astype(BF16), _pad_heads(w_br_swa[l], axis=0).astype(BF16),
                    w_out[l].astype(BF16))
        xf = _ffn(xf, row(ffn2_pre_g[l]), row(ffn2_post_g[l]), ffn2_w_gate[l].astype(BF16),
                  ffn2_w_up[l].astype(BF16), ffn2_w_down[l].astype(BF16))
    return xf.reshape(B, S, D)
```

```python
import functools
import math

import numpy as np
import jax
import jax.numpy as jnp
from jax import lax
from jax.experimental import pallas as pl
from jax.experimental.pallas import tpu as pltpu

F32 = jnp.float32
BF16 = jnp.bfloat16
HIGHEST = lax.Precision.HIGHEST

D_MODEL = 1024
HEAD_DIM = 64
HP = 128
ROPE_THETA = 10000.0
RMS_EPS = 1e-6
GN_EPS = 1e-5
RWKV_LN_EPS = 64e-5
D_FF = 2816
NEG_INF = -1e30
FORCED_SCORE = 1e9

NSA_HEADS = 4
NSA_CMP_LEN = 32
NSA_CMP_STRIDE = 16
NSA_CMP_HIDDEN = 256
NSA_SEL_BLOCK = 64
NSA_TOP_N = 16
NSA_WINDOW = 512
RET_HEADS = 4
RET_DV = 128
RET_CHUNK = 128
RWKV_HEADS = 4
RWKV_CHUNK = 64
SWA_HEADS = 4
SWA_KV_HEADS = 2
SWA_WINDOW = 128
ATT_BLOCK = 128
SLC_TILE = 512

VMEM_LIMIT = 56 * 1024 * 1024


def _cp(*sem):
    return pltpu.CompilerParams(dimension_semantics=sem, vmem_limit_bytes=VMEM_LIMIT)


def _const_spec(shape):
    nd = len(shape)
    return pl.BlockSpec(shape, lambda *_: (0,) * nd, pipeline_mode=pl.Buffered(1))


def _dot(a, b):
    return jnp.dot(a.astype(BF16), b.astype(BF16), preferred_element_type=F32)


def _dot_nt(a, b):
    return lax.dot_general(a.astype(BF16), b.astype(BF16), (((1,), (1,)), ((), ())),
                           preferred_element_type=F32)


def _dot_tn(a, b):
    return lax.dot_general(a.astype(BF16), b.astype(BF16), (((0,), (0,)), ((), ())),
                           preferred_element_type=F32)


def _dot_hi(a, b):
    return jnp.dot(a, b, precision=HIGHEST, preferred_element_type=F32)


def _rms(x, g):
    return x * lax.rsqrt(jnp.mean(x * x, axis=-1, keepdims=True) + RMS_EPS) * g


def _rope(t, cos, sin_lo, sin_hi):
    return t * cos + pltpu.roll(t, 32, 1) * sin_hi - pltpu.roll(t, HP - 32, 1) * sin_lo


def _ffn_kernel(x_ref, pre_ref, post_ref, wg_ref, wu_ref, wd_ref, o_ref, *, ck):
    x = x_ref[...]
    h = _rms(x, pre_ref[...]).astype(BF16)
    acc = jnp.zeros(x.shape, F32)
    for c in range(D_FF // ck):
        g = jnp.dot(h, wg_ref[:, c * ck:(c + 1) * ck], preferred_element_type=F32)
        u = jnp.dot(h, wu_ref[:, c * ck:(c + 1) * ck], preferred_element_type=F32)
        a = (jax.nn.silu(g) * u).astype(BF16)
        acc = acc + jnp.dot(a, wd_ref[c * ck:(c + 1) * ck, :], preferred_element_type=F32)
    o_ref[...] = x + 0.5 * _rms(acc, post_ref[...])


def _ffn(x, pre_g, post_g, wg, wu, wd, tm=512):
    T, D = x.shape
    return pl.pallas_call(
        functools.partial(_ffn_kernel, ck=256),
        grid=(T // tm,),
        in_specs=[pl.BlockSpec((tm, D), lambda i: (i, 0)),
                  _const_spec((1, D)), _const_spec((1, D)),
                  _const_spec((D, D_FF)), _const_spec((D, D_FF)), _const_spec((D_FF, D))],
        out_specs=pl.BlockSpec((tm, D), lambda i: (i, 0)),
        out_shape=jax.ShapeDtypeStruct((T, D), F32),
        compiler_params=_cp("arbitrary"),
    )(x, pre_g, post_g, wg, wu, wd)


RW_COLS = 3 * 4 * HP + 3 * HP
RW_OUT = 7 * 4 * HP


def _inproj_kernel(x_ref, pre_ref, rope_ref, wn_ref, wr_ref, ww_ref, ws_ref,
                   mu_ref, w0_ref, w2_ref, a0_ref, a2_ref, g2_ref, kk_ref, ka_ref,
                   nq_ref, nqr_ref, nslc_ref, nwin_ref, ncmp_ref,
                   rq_ref, rk_ref, rv_ref, rg_ref, rw_ref, sq_ref, skv_ref,
                   carry_ref, *, tiles_per_seq):
    i = pl.program_id(0)
    x = x_ref[...]
    tm = x.shape[0]
    h = _rms(x, pre_ref[...]).astype(BF16)
    cos = rope_ref[:, 0:HP]
    s_lo = rope_ref[:, HP:2 * HP]
    s_hi = rope_ref[:, 2 * HP:3 * HP]
    rope = lambda t: _rope(t, cos, s_lo, s_hi)
    hs = lambda t, j: t[:, j * HP:(j + 1) * HP]

    p = jnp.dot(h, wn_ref[...], preferred_element_type=F32)
    nq_ref[...] = p[:, 0:4 * HP].astype(BF16)
    for j in range(4):
        nqr_ref[:, j * HP:(j + 1) * HP] = rope(hs(p, j)).astype(BF16)
    nslc_ref[:, 0:HP] = rope(hs(p, 4)).astype(BF16)
    nslc_ref[:, HP:2 * HP] = hs(p, 5).astype(BF16)
    nwin_ref[:, 0:HP] = rope(hs(p, 6)).astype(BF16)
    nwin_ref[:, HP:2 * HP] = hs(p, 7).astype(BF16)
    ncmp_ref[...] = hs(p, 8)

    p = jnp.dot(h, wr_ref[...], preferred_element_type=F32)
    for j in range(4):
        rq_ref[:, j * HP:(j + 1) * HP] = rope(hs(p, j))
        rk_ref[:, j * HP:(j + 1) * HP] = rope(hs(p, 4 + j))
    rv_ref[...] = p[:, 8 * HP:8 * HP + 512].astype(BF16)
    rg_ref[...] = p[:, 8 * HP + 512:8 * HP + 1024]

    p = jnp.dot(h, ws_ref[...], preferred_element_type=F32)
    for j in range(4):
        sq_ref[:, j * HP:(j + 1) * HP] = rope(hs(p, j)).astype(BF16)
    for j in range(2):
        skv_ref[:, j * HP:(j + 1) * HP] = rope(hs(p, 4 + j)).astype(BF16)
    skv_ref[:, 2 * HP:4 * HP] = p[:, 6 * HP:8 * HP].astype(BF16)

    p = jnp.dot(h, ww_ref[...], preferred_element_type=F32)
    @pl.when(i == 0)
    def _():
        carry_ref[...] = jnp.zeros(carry_ref.shape, F32)

    first = jnp.where(i % tiles_per_seq == 0, 0.0, carry_ref[0:1, :])
    row = lax.broadcasted_iota(jnp.int32, p.shape, 0)
    prev = jnp.where(row == 0, first, pltpu.roll(p, 1, 0))
    carry_ref[0:1, :] = p[tm - 1:tm, :]
    xm = p + mu_ref[...] * (prev - p)
    W4 = 4 * HP
    r = xm[:, 0:W4]
    k = xm[:, W4:2 * W4]
    v = xm[:, 2 * W4:3 * W4]
    wl = xm[:, 3 * W4:3 * W4 + HP]
    al = xm[:, 3 * W4 + HP:3 * W4 + 2 * HP]
    gl = xm[:, 3 * W4 + 2 * HP:3 * W4 + 3 * HP]
    z = -(w0_ref[...] + _dot(jnp.tanh(wl), w2_ref[...]))
    softplus = jnp.maximum(z, 0.0) + jnp.log(1.0 + jnp.exp(-jnp.abs(z)))
    w = -softplus - 0.5
    lane = lax.broadcasted_iota(jnp.int32, w.shape, 1)
    logw = jnp.where(lane % HP < HEAD_DIM, -jnp.exp(w), 0.0)
    a = jax.nn.sigmoid(a0_ref[...] + _dot(al, a2_ref[...]))
    g = _dot(jax.nn.sigmoid(gl), g2_ref[...])
    kkr = k * kk_ref[...]
    k2 = k * (1.0 + (a - 1.0) * ka_ref[...])
    rw_ref[:, 0:W4] = r
    rw_ref[:, W4:2 * W4] = k2
    rw_ref[:, 2 * W4:3 * W4] = v
    for j in range(4):
        kj = hs(kkr, j)
        nrm = jnp.maximum(jnp.sqrt(jnp.sum(kj * kj, axis=-1, keepdims=True)), 1e-12)
        kkj = kj / nrm
        rw_ref[:, 3 * W4 + j * HP:3 * W4 + (j + 1) * HP] = -kkj
        rw_ref[:, 4 * W4 + j * HP:4 * W4 + (j + 1) * HP] = kkj * hs(a, j)
    rw_ref[:, 5 * W4:6 * W4] = logw
    rw_ref[:, 6 * W4:7 * W4] = g


def _inproj(x, S, pre_g, rope_tab, wn, wr, ww, ws, mu, w0, w2, a0, a2, g2, k_k, k_a, tm=256):
    T, D = x.shape
    row = lambda w: pl.BlockSpec((tm, w), lambda i: (i, 0))
    tps = S // tm
    outs = [4 * HP, 4 * HP, 2 * HP, 2 * HP, HP, 4 * HP, 4 * HP, 512, 512, RW_OUT, 4 * HP, 4 * HP]
    dts = [BF16, BF16, BF16, BF16, F32, F32, F32, BF16, F32, F32, BF16, BF16]
    return pl.pallas_call(
        functools.partial(_inproj_kernel, tiles_per_seq=tps),
        grid=(T // tm,),
        in_specs=[row(D), _const_spec((1, D)),
                  pl.BlockSpec((tm, 3 * HP), lambda i: (i % tps, 0)),
                  _const_spec(wn.shape), _const_spec(wr.shape), _const_spec(ww.shape), _const_spec(ws.shape),
                  _const_spec(mu.shape), _const_spec(w0.shape), _const_spec(w2.shape), _const_spec(a0.shape),
                  _const_spec(a2.shape), _const_spec(g2.shape), _const_spec(k_k.shape), _const_spec(k_a.shape)],
        out_specs=[row(w) for w in outs],
        out_shape=[jax.ShapeDtypeStruct((T, w), dt) for w, dt in zip(outs, dts)],
        scratch_shapes=[pltpu.VMEM((8, RW_COLS), F32)],
        compiler_params=_cp("arbitrary"),
    )(x, pre_g, rope_tab, wn, wr, ww, ws, mu, w0, w2, a0, a2, g2, k_k, k_a)


def _compress_kernel(h_ref, add_ref, w1_ref, w2_ref, kc_ref, vc_ref):
    hb = h_ref[...]
    n_half = hb.shape[0]
    for t, o_ref in ((0, kc_ref), (1, vc_ref)):
        top = _dot(hb + add_ref[2 * t:2 * t + 1, :], w1_ref[2 * t])
        bot = _dot(hb + add_ref[2 * t + 1:2 * t + 2, :], w1_ref[2 * t + 1])
        pre = top + pltpu.roll(bot, n_half - 1, 0)
        o_ref[...] = _dot(jax.nn.gelu(pre), w2_ref[t])


def _compress(hb, add, w1, w2):
    B, n_half, W = hb.shape
    return pl.pallas_call(
        _compress_kernel,
        grid=(B,),
        in_specs=[pl.BlockSpec((None, n_half, W), lambda b: (b, 0, 0)),
                  _const_spec(add.shape), _const_spec(w1.shape), _const_spec(w2.shape)],
        out_specs=[pl.BlockSpec((None, n_half, HP), lambda b: (b, 0, 0))] * 2,
        out_shape=[jax.ShapeDtypeStruct((B, n_half, HP), F32)] * 2,
        compiler_params=_cp("arbitrary"),
    )(hb, add, w1, w2)


def _cmp_kernel(q_ref, kc_ref, vc_ref, ovl_ref, o_ref, sel_ref, *, n_top):
    i = pl.program_id(1)
    tq = q_ref.shape[0]
    n_half = kc_ref.shape[0]
    n_sel = ovl_ref.shape[0]
    kc = kc_ref[...]
    vc = vc_ref[...]
    tpos = i * tq + lax.broadcasted_iota(jnp.int32, (tq, n_half), 0)
    n_id = lax.broadcasted_iota(jnp.int32, (tq, n_half), 1)
    mask = (n_id * NSA_CMP_STRIDE + NSA_CMP_LEN - 1 <= tpos) & (n_id < n_half - 1)
    maskf = mask.astype(F32)
    heads = range(NSA_HEADS)
    s = [jnp.where(mask, _dot_nt(q_ref[:, hd * HP:(hd + 1) * HP], kc), NEG_INF) for hd in heads]
    m = [jnp.max(s[hd], axis=-1, keepdims=True) for hd in heads]
    p = [jnp.exp(s[hd] - m[hd]) * maskf for hd in heads]
    p = [p[hd] / jnp.maximum(jnp.sum(p[hd], axis=-1, keepdims=True), 1e-30) for hd in heads]
    o = [_dot(p[hd], vc) for hd in heads]
    imp = [_dot_nt(ovl_ref[...], p[hd]) for hd in heads]
    imp_t = (imp[0] + imp[1]) + (imp[2] + imp[3])
    for hd in heads:
        o_ref[:, hd * HP:(hd + 1) * HP] = o[hd]
    blk = lax.broadcasted_iota(jnp.int32, (n_sel, tq), 0)
    cur = (i * tq + lax.broadcasted_iota(jnp.int32, (n_sel, tq), 1)) // NSA_SEL_BLOCK
    forced = (blk == 0) | (blk == cur) | (blk == cur - 1)
    valid = blk <= cur
    score = jnp.where(forced, FORCED_SCORE, jnp.where(valid, imp_t, -FORCED_SCORE))
    rank = jnp.zeros((n_sel, tq), F32)
    for b in range(n_sel):
        sb = score[b:b + 1, :]
        before = (sb > score) | ((sb == score) & (blk > b))
        rank = rank + before.astype(F32)
    sel_t = ((rank < n_top) & valid).astype(F32)
    sel_ref[...] = sel_t.T


def _cmp_attn(q, kc, vc, ovl_t, n_top, tq=ATT_BLOCK):
    B, S, W = q.shape
    n_half = kc.shape[1]
    n_sel = ovl_t.shape[0]
    return pl.pallas_call(
        functools.partial(_cmp_kernel, n_top=n_top),
        grid=(B, S // tq),
        in_specs=[pl.BlockSpec((None, tq, W), lambda b, i: (b, i, 0)),
                  pl.BlockSpec((None, n_half, HP), lambda b, i: (b, 0, 0)),
                  pl.BlockSpec((None, n_half, HP), lambda b, i: (b, 0, 0)),
                  _const_spec(ovl_t.shape)],
        out_specs=[pl.BlockSpec((None, tq, W), lambda b, i: (b, i, 0)),
                   pl.BlockSpec((None, tq, n_sel), lambda b, i: (b, i, 0))],
        out_shape=[jax.ShapeDtypeStruct((B, S, W), F32), jax.ShapeDtypeStruct((B, S, n_sel), F32)],
        compiler_params=_cp("arbitrary", "arbitrary"),
    )(q, kc, vc, ovl_t)


def _band_kernel(*refs, n_kv, group, window, n_prev, use_sink):
    if use_sink:
        q_ref, kv_ref, sink_ref, o_ref = refs
    else:
        q_ref, kv_ref, o_ref = refs
    i = pl.program_id(1)
    tq = q_ref.shape[0]
    span = (n_prev + 1) * tq
    start = pl.multiple_of(jnp.maximum(i - n_prev, 0) * tq, tq)
    tpos = i * tq + lax.broadcasted_iota(jnp.int32, (tq, span), 0)
    diff = tpos - (start + lax.broadcasted_iota(jnp.int32, (tq, span), 1))
    mask = (diff >= 0) & (diff < window)
    heads = range(n_kv * group)
    hsl = lambda hd: slice(hd * HP, (hd + 1) * HP)
    k = [kv_ref[pl.ds(start, span), hsl(kh)] for kh in range(n_kv)]
    v = [kv_ref[pl.ds(start, span), hsl(n_kv + kh)] for kh in range(n_kv)]
    s = [jnp.where(mask, _dot_nt(q_ref[:, hsl(hd)], k[hd // group]), NEG_INF) for hd in heads]
    m = [jnp.max(s[hd], axis=-1, keepdims=True) for hd in heads]
    if use_sink:
        m = [jnp.maximum(m[hd], sink_ref[hd]) for hd in heads]
    p = [jnp.exp(s[hd] - m[hd]) for hd in heads]
    l = [jnp.sum(p[hd], axis=-1, keepdims=True) for hd in heads]
    if use_sink:
        l = [l[hd] + jnp.exp(sink_ref[hd] - m[hd]) for hd in heads]
    o = [_dot(p[hd], v[hd // group]) for hd in heads]
    for hd in heads:
        o_ref[:, hsl(hd)] = (o[hd] / l[hd]).astype(o_ref.dtype)


def _band(q, kv, n_kv, group, window, out_dtype, sinks=None, tq=ATT_BLOCK):
    B, S, W = q.shape
    n_prev = -(-(window - 1) // tq)
    assert S >= (n_prev + 1) * tq
    use_sink = sinks is not None
    args = [q, kv]
    in_specs = [pl.BlockSpec((None, tq, W), lambda b, i: (b, i, 0)),
                pl.BlockSpec((None, S, kv.shape[2]), lambda b, i: (b, 0, 0))]
    if use_sink:
        args.append(sinks)
        in_specs.append(pl.BlockSpec(memory_space=pltpu.SMEM))
    return pl.pallas_call(
        functools.partial(_band_kernel, n_kv=n_kv, group=group, window=window, n_prev=n_prev, use_sink=use_sink),
        grid=(B, S // tq),
        in_specs=in_specs,
        out_specs=pl.BlockSpec((None, tq, W), lambda b, i: (b, i, 0)),
        out_shape=jax.ShapeDtypeStruct((B, S, W), out_dtype),
        compiler_params=_cp("arbitrary", "arbitrary"),
    )(*args)


def _slc_kernel(q_ref, kv_ref, sel_ref, exp_ref, o_ref):
    i = pl.program_id(1)
    tq = q_ref.shape[0]
    tk = exp_ref.shape[2]
    n_t = (i * tq + tq - 1) // tk + 1
    tpos = i * tq + lax.broadcasted_iota(jnp.int32, (tq, tk), 0)
    kcol = lax.broadcasted_iota(jnp.int32, (tq, tk), 1)
    selm = sel_ref[...].astype(BF16)
    heads = range(NSA_HEADS)
    q = [q_ref[:, hd * HP:(hd + 1) * HP] for hd in heads]

    def body(j, carry):
        r0 = pl.multiple_of(j * tk, tk)
        k_t = kv_ref[pl.ds(r0, tk), 0:HP]
        v_t = kv_ref[pl.ds(r0, tk), HP:2 * HP]
        ms = jnp.dot(selm, exp_ref[j], preferred_element_type=F32)
        bias = jnp.where((ms > 0.5) & (tpos - (j * tk + kcol) >= 0), 0.0, NEG_INF)
        s = [_dot_nt(q[hd], k_t) + bias for hd in heads]
        m_new = [jnp.maximum(carry[hd][0], jnp.max(s[hd], axis=-1, keepdims=True)) for hd in heads]
        p = [jnp.exp(s[hd] - m_new[hd]) for hd in heads]
        pv = [_dot(p[hd], v_t) for hd in heads]
        out = []
        for hd in heads:
            m, l, acc = carry[hd]
            alpha = jnp.exp(m - m_new[hd])
            out.append((m_new[hd], l * alpha + jnp.sum(p[hd], axis=-1, keepdims=True), acc * alpha + pv[hd]))
        return tuple(out)

    init = tuple((jnp.full((tq, 1), NEG_INF, F32), jnp.zeros((tq, 1), F32), jnp.zeros((tq, HP), F32))
                 for _ in heads)
    fin = lax.fori_loop(0, n_t, body, init)
    for hd in heads:
        o_ref[:, hd * HP:(hd + 1) * HP] = fin[hd][2] / fin[hd][1]


def _slc(q, kv, sel, expand, tq=ATT_BLOCK):
    B, S, W = q.shape
    return pl.pallas_call(
        _slc_kernel,
        grid=(B, S // tq),
        in_specs=[pl.BlockSpec((None, tq, W), lambda b, i: (b, i, 0)),
                  pl.BlockSpec((None, S, kv.shape[2]), lambda b, i: (b, 0, 0)),
                  pl.BlockSpec((None, tq, sel.shape[2]), lambda b, i: (b, i, 0)),
                  _const_spec(expand.shape)],
        out_specs=pl.BlockSpec((None, tq, W), lambda b, i: (b, i, 0)),
        out_shape=jax.ShapeDtypeStruct((B, S, W), F32),
        compiler_params=_cp("arbitrary", "arbitrary"),
    )(q, kv, sel, expand)


def _ret_kernel(q_ref, k_ref, v_ref, g_ref, dm_ref, zeta_ref, xi_ref, gch_ref, gn_ref, o_ref, st_ref):
    c = pl.program_id(1)

    @pl.when(c == 0)
    def _():
        st_ref[...] = jnp.zeros(st_ref.shape, F32)

    for hd in range(RET_HEADS):
        q = q_ref[:, hd * HP:(hd + 1) * HP]
        k = k_ref[:, hd * HP:(hd + 1) * HP]
        v = v_ref[:, hd * RET_DV:(hd + 1) * RET_DV]
        inner = _dot_nt(q, k) * dm_ref[hd]
        o = _dot(inner, v)
        state = st_ref[hd]
        o = o + _dot(q * xi_ref[hd], state)
        st_ref[hd] = state * gch_ref[hd] + _dot_tn(k * zeta_ref[hd], v)
        mu = jnp.mean(o, axis=-1, keepdims=True)
        d = o - mu
        var = jnp.mean(d * d, axis=-1, keepdims=True)
        on = d * lax.rsqrt(var + GN_EPS) * gn_ref[:, hd * RET_DV:(hd + 1) * RET_DV]
        o_ref[:, hd * RET_DV:(hd + 1) * RET_DV] = (jax.nn.silu(g_ref[:, hd * RET_DV:(hd + 1) * RET_DV]) * on).astype(BF16)


def _retention(q, k, v, g, dmask, zeta, xi, gch, gn_g):
    B, S, _ = q.shape
    C = RET_CHUNK
    blk = lambda w: pl.BlockSpec((None, C, w), lambda b, c: (b, c, 0))
    return pl.pallas_call(
        _ret_kernel,
        grid=(B, S // C),
        in_specs=[blk(4 * HP), blk(4 * HP), blk(512), blk(512),
                  _const_spec(dmask.shape), _const_spec(zeta.shape), _const_spec(xi.shape),
                  _const_spec(gch.shape), _const_spec(gn_g.shape)],
        out_specs=blk(512),
        out_shape=jax.ShapeDtypeStruct((B, S, 512), BF16),
        scratch_shapes=[pltpu.VMEM((RET_HEADS, HP, RET_DV), F32)],
        compiler_params=_cp("arbitrary", "arbitrary"),
    )(q, k, v, g, dmask, zeta, xi, gch, gn_g)


def _rwkv_kernel(rw_ref, rk_ref, lng_ref, lnb_ref, o_ref, z_ref, *, n_chunks):
    c = pl.program_id(1)

    @pl.when(c == 0)
    def _():
        z_ref[...] = jnp.zeros(z_ref.shape, F32)

    L = RWKV_CHUNK
    W4 = 4 * HP
    ri = lax.broadcasted_iota(jnp.int32, (2 * L, 2 * L), 0)
    ci = lax.broadcasted_iota(jnp.int32, (2 * L, 2 * L), 1) % L
    a_mask = ((ri < L) & (ri > ci)) | (ri - L >= ci)
    tri = (lax.broadcasted_iota(jnp.int32, (L, L), 0) >= lax.broadcasted_iota(jnp.int32, (L, L), 1)).astype(BF16)
    dr = lax.broadcasted_iota(jnp.int32, (HP, HP), 0)
    dc = lax.broadcasted_iota(jnp.int32, (HP, HP), 1)
    real = (lax.broadcasted_iota(jnp.int32, (L, HP), 1) < HEAD_DIM).astype(F32)
    zeros = jnp.zeros((L, HP), F32)
    tdot = lambda t: jnp.dot(tri, t, preferred_element_type=F32)
    pairs = [(ch, hd) for ch in range(n_chunks) for hd in range(RWKV_HEADS)]
    col = lambda ch, hd, gi: rw_ref[ch * L:(ch + 1) * L, gi * W4 + hd * HP:gi * W4 + (hd + 1) * HP]
    cum_all = []
    for ch in range(n_chunks):
        lw_all = rw_ref[ch * L:(ch + 1) * L, 5 * W4:6 * W4]
        hi = lw_all.astype(BF16)
        rem = lw_all - hi.astype(F32)
        mid = rem.astype(BF16)
        lo = (rem - mid.astype(F32)).astype(BF16)
        cum_all.append(tdot(hi) + tdot(mid) + tdot(lo))
    cum = {p: cum_all[p[0]][:, p[1] * HP:(p[1] + 1) * HP] for p in pairs}
    r, k, v, a, b, lw = ({p: col(p[0], p[1], gi) for p in pairs} for gi in range(6))
    e_inv = {p: jnp.exp(-cum[p]) for p in pairs}
    at = {p: a[p] * jnp.exp(cum[p] - lw[p]) for p in pairs}
    rt = {p: r[p] * jnp.exp(cum[p]) for p in pairs}
    amat = {p: jnp.where(a_mask, _dot_nt(jnp.concatenate([at[p], rt[p]], axis=0),
                                         jnp.concatenate([b[p] * e_inv[p], k[p] * e_inv[p]], axis=0)), 0.0)
            for p in pairs}
    x = {p: jnp.concatenate([at[p], _dot(amat[p][0:L], jnp.concatenate([zeros, v[p]], axis=0))], axis=1)
         for p in pairs}
    npow = {p: amat[p][0:L, 0:L] for p in pairs}
    for s in range(int(math.log2(L))):
        x = {p: x[p] + _dot(npow[p], x[p]) for p in pairs}
        if (2 << s) < L:
            npow = {p: _dot(npow[p], npow[p]) for p in pairs}
    rhs = {p: jnp.concatenate([x[p], jnp.concatenate([zeros, v[p]], axis=1)], axis=0) for p in pairs}
    yq = {p: _dot(amat[p][L:2 * L], rhs[p]) for p in pairs}
    mn = {}
    for p in pairs:
        e_rem = jnp.exp(cum[p][L - 1:L, :] - cum[p])
        mn[p] = _dot_tn(jnp.concatenate([b[p] * e_rem, k[p] * e_rem], axis=0), rhs[p])
    y = {}
    for hd in range(RWKV_HEADS):
        z = z_ref[hd]
        for ch in range(n_chunks):
            p = (ch, hd)
            y[p] = yq[p][:, HP:2 * HP] + _dot(rt[p] + yq[p][:, 0:HP], z)
            decay = jnp.where(dr == dc, jnp.exp(cum[p][L - 1:L, :]), 0.0)
            z = _dot(decay + mn[p][:, 0:HP], z) + mn[p][:, HP:2 * HP]
        z_ref[hd] = z
    for p in pairs:
        ch, hd = p
        mu = jnp.sum(y[p], axis=-1, keepdims=True) * (1.0 / HEAD_DIM)
        d = (y[p] - mu) * real
        var = jnp.sum(d * d, axis=-1, keepdims=True) * (1.0 / HEAD_DIM)
        yn = d * lax.rsqrt(var + RWKV_LN_EPS) * lng_ref[:, hd * HP:(hd + 1) * HP] + lnb_ref[:, hd * HP:(hd + 1) * HP]
        bonus = jnp.sum(r[p] * k[p] * rk_ref[:, hd * HP:(hd + 1) * HP], axis=-1, keepdims=True) * v[p]
        o_ref[ch * L:(ch + 1) * L, hd * HP:(hd + 1) * HP] = ((yn + bonus) * col(ch, hd, 6)).astype(o_ref.dtype)


def _rwkv(rw, r_k, ln_g, ln_b, n_chunks=2):
    B, S, _ = rw.shape
    L = n_chunks * RWKV_CHUNK
    return pl.pallas_call(
        functools.partial(_rwkv_kernel, n_chunks=n_chunks),
        grid=(B, S // L),
        in_specs=[pl.BlockSpec((None, L, RW_OUT), lambda b, c: (b, c, 0)),
                  _const_spec(r_k.shape), _const_spec(ln_g.shape), _const_spec(ln_b.shape)],
        out_specs=pl.BlockSpec((None, L, 4 * HP), lambda b, c: (b, c, 0)),
        out_shape=jax.ShapeDtypeStruct((B, S, 4 * HP), BF16),
        scratch_shapes=[pltpu.VMEM((RWKV_HEADS, HP, HP), F32)],
        compiler_params=_cp("arbitrary", "arbitrary"),
    )(rw, r_k, ln_g, ln_b)


def _merge_kernel(x_ref, ocmp_ref, oslc_ref, owin_ref, yret_ref, yrw_ref, yswa_ref,
                  pre_ref, post_ref, wg_ref, wn_ref, wr_ref, ww_ref, ws_ref, wo_ref, o_ref):
    x = x_ref[...]
    D = x.shape[1]
    W4 = 4 * HP
    h = _rms(x, pre_ref[...]).astype(BF16)
    gate = lambda j, w: jax.nn.sigmoid(jnp.dot(h, wg_ref[:, j:j + w], preferred_element_type=F32))
    y_nsa = (gate(4 * D, W4) * ocmp_ref[...] + gate(4 * D + W4, W4) * oslc_ref[...]
             + gate(4 * D + 2 * W4, W4) * owin_ref[...])
    merged = gate(0, D) * _dot(y_nsa, wn_ref[...])
    merged = merged + gate(D, D) * _dot(yret_ref[...], wr_ref[...])
    merged = merged + gate(2 * D, D) * _dot(yrw_ref[...], ww_ref[...])
    merged = merged + gate(3 * D, D) * _dot(yswa_ref[...], ws_ref[...])
    o_ref[...] = x + _rms(_dot(merged, wo_ref[...]), post_ref[...])


def _merge(x, ocmp, oslc, owin, yret, yrw, yswa, pre_g, post_g, wg, wn, wr, ww, ws, wo, tm=256):
    T, D = x.shape
    row = lambda w: pl.BlockSpec((tm, w), lambda i: (i, 0))
    return pl.pallas_call(
        _merge_kernel,
        grid=(T // tm,),
        in_specs=[row(D), row(4 * HP), row(4 * HP), row(4 * HP), row(512), row(4 * HP), row(4 * HP),
                  _const_spec((1, D)), _const_spec((1, D)), _const_spec(wg.shape), _const_spec(wn.shape),
                  _const_spec(wr.shape), _const_spec(ww.shape), _const_spec(ws.shape), _const_spec(wo.shape)],
        out_specs=row(D),
        out_shape=jax.ShapeDtypeStruct((T, D), F32),
        compiler_params=_cp("arbitrary"),
    )(x, ocmp, oslc, owin, yret, yrw, yswa, pre_g, post_g, wg, wn, wr, ww, ws, wo)


def _column_layout():
    spec = (
        ('nsa_q', 256), ('nsa_k_cmp', 64), ('nsa_v_cmp', 64), ('nsa_k_slc', 64), ('nsa_v_slc', 64),
        ('nsa_k_win', 64), ('nsa_v_win', 64), ('nsa_gate', 12),
        ('ret_q', 256), ('ret_k', 256), ('ret_v', 512), ('ret_g', 512),
        ('rwkv', 1024),
        ('swa_q', 256), ('swa_k', 128), ('swa_v', 128),
        ('branch_gate', 4 * D_MODEL),
    )
    layout, start = {}, 0
    for name, width in spec:
        layout[name] = (start, start + width)
        start += width
    return layout


def _pad_heads(w, axis=-1):
    w = jnp.moveaxis(w, axis, -1)
    lead = w.shape[:-1]
    n = w.shape[-1] // HEAD_DIM
    w = w.reshape(lead + (n, HEAD_DIM))
    w = jnp.pad(w, [(0, 0)] * len(lead) + [(0, 0), (0, HP - HEAD_DIM)])
    return jnp.moveaxis(w.reshape(lead + (n * HP,)), -1, axis)


def _pad_cols(w, width):
    return jnp.pad(w, [(0, 0)] * (w.ndim - 1) + [(0, width - w.shape[-1])])


def _rope_table(S):
    half = HEAD_DIM // 2
    inv_freq = jnp.power(ROPE_THETA, -jnp.arange(half, dtype=F32) * 2.0 / HEAD_DIM)
    ang = jnp.arange(S, dtype=jnp.int32).astype(F32)[:, None] * inv_freq[None, :]
    cos, sin, z = jnp.cos(ang), jnp.sin(ang), jnp.zeros((S, half), F32)
    zp = jnp.zeros((S, HP - HEAD_DIM), F32)
    return jnp.concatenate([cos, cos, zp, sin, z, zp, z, sin, zp], axis=1)


def _retention_tables():
    H, C = RET_HEADS, RET_CHUNK
    log_g = jnp.log(1.0 - jnp.power(2.0, -5.0 - jnp.arange(H, dtype=F32)))
    i = jnp.arange(C, dtype=F32)
    diff = i[:, None] - i[None, :]
    dmask = jnp.where(diff >= 0, jnp.exp(log_g[:, None, None] * jnp.maximum(diff, 0.0)), 0.0)
    zeta = jnp.exp(log_g[:, None] * (C - 1 - i)[None, :])
    xi = jnp.exp(log_g[:, None] * (i + 1.0)[None, :])
    gch = jnp.exp(log_g * C)
    bc = lambda t: jnp.broadcast_to(t[:, :, None], (H, C, HP))
    return dmask, bc(zeta), bc(xi), jnp.broadcast_to(gch[:, None, None], (H, HP, RET_DV))


def kernel(x, ffn1_pre_g, ffn1_post_g, ffn1_w_gate, ffn1_w_up, ffn1_w_down, mix_pre_g, mix_post_g, w_in, nsa_cmp_pos_k, nsa_cmp_pos_v, nsa_cmp_k_w1, nsa_cmp_k_w2, nsa_cmp_v_w1, nsa_cmp_v_w2, ret_gn_g, rwkv_mu, rwkv_w0, rwkv_w2, rwkv_a0, rwkv_a2, rwkv_g2, rwkv_k_k, rwkv_k_a, rwkv_r_k, rwkv_ln_g, rwkv_ln_b, swa_sinks, w_br_nsa, w_br_ret, w_br_rwkv, w_br_swa, w_out, ffn2_pre_g, ffn2_post_g, ffn2_w_gate, ffn2_w_up, ffn2_w_down):
    B, S, D = x.shape
    T = B * S
    depth = w_in.shape[0]
    lay = _column_layout()
    n_half = S // NSA_CMP_STRIDE
    n_sel = S // NSA_SEL_BLOCK
    n_top = min(NSA_TOP_N, n_sel)
    scale = HEAD_DIM ** -0.5
    rope_tab = _rope_table(S)
    dmask, zeta, xi, gch = _retention_tables()
    cs = np.arange(n_half) * NSA_CMP_STRIDE
    ss = np.arange(n_sel) * NSA_SEL_BLOCK
    ovl = ((cs[None, :] <= ss[:, None] + NSA_SEL_BLOCK - 1) & (cs[None, :] + NSA_CMP_LEN - 1 >= ss[:, None])
           & (np.arange(n_half)[None, :] < n_half - 1))
    ovl_t = jnp.asarray(ovl, BF16)
    key_blk = (np.arange(S) // NSA_SEL_BLOCK).reshape(S // SLC_TILE, 1, SLC_TILE)
    expand = jnp.asarray(key_blk == np.arange(n_sel)[None, :, None], BF16)
    row = lambda t: t.reshape(1, -1)

    xf = x.reshape(T, D)
    for l in range(depth):
        wl = w_in[l]
        c = lambda name: wl[:, lay[name][0]:lay[name][1]]
        xf = _ffn(xf, row(ffn1_pre_g[l]), row(ffn1_post_g[l]), ffn1_w_gate[l].astype(BF16),
                  ffn1_w_up[l].astype(BF16), ffn1_w_down[l].astype(BF16))
        w_nsa = jnp.concatenate([_pad_heads(c('nsa_q') * scale), _pad_cols(c('nsa_k_slc'), HP),
                                 _pad_cols(c('nsa_v_slc'), HP), _pad_cols(c('nsa_k_win'), HP),
                                 _pad_cols(c('nsa_v_win'), HP), c('nsa_k_cmp'), c('nsa_v_cmp')], axis=1).astype(BF16)
        w_ret = jnp.concatenate([_pad_heads(c('ret_q')), _pad_heads(c('ret_k') * scale),
                                 c('ret_v'), c('ret_g')], axis=1).astype(BF16)
        rw0 = lay['rwkv'][0]
        rcol = lambda s, e: wl[:, rw0 + s:rw0 + e]
        w_rw = jnp.concatenate([_pad_heads(rcol(0, 256)), _pad_heads(rcol(256, 512)), _pad_heads(rcol(512, 768)),
                                _pad_cols(rcol(768, 832), HP), _pad_cols(rcol(832, 896), HP), rcol(896, 1024)],
                               axis=1).astype(BF16)
        w_swa = jnp.concatenate([_pad_heads(c('swa_q') * scale), _pad_heads(c('swa_k')),
                                 _pad_heads(c('swa_v'))], axis=1).astype(BF16)
        mu = rwkv_mu[l]
        mu_p = row(jnp.concatenate([_pad_heads(mu[0:256]), _pad_heads(mu[256:512]), _pad_heads(mu[512:768]),
                                    _pad_cols(mu[768:832], HP), _pad_cols(mu[832:896], HP), mu[896:1024]]))
        w2_p = jnp.pad(_pad_heads(rwkv_w2[l]), ((0, HP - rwkv_w2.shape[1]), (0, 0)))
        a2_p = jnp.pad(_pad_heads(rwkv_a2[l]), ((0, HP - rwkv_a2.shape[1]), (0, 0)))
        g2_p = _pad_heads(rwkv_g2[l])
        (nsa_q, nsa_qr, kv_slc, kv_win, kv_cmp, ret_q, ret_k, ret_v, ret_g, rw, swa_q, swa_kv) = _inproj(
            xf, S, row(mix_pre_g[l]), rope_tab, w_nsa, w_ret, w_rw, w_swa, mu_p,
            row(_pad_heads(rwkv_w0[l])), w2_p, row(_pad_heads(rwkv_a0[l])), a2_p, g2_p,
            row(_pad_heads(rwkv_k_k[l])), row(_pad_heads(rwkv_k_a[l])))
        b3 = lambda t: t.reshape(B, S, t.shape[-1])

        half = NSA_CMP_STRIDE * HEAD_DIM
        z16 = jnp.zeros((NSA_CMP_STRIDE, HEAD_DIM), F32)
        emb = lambda pk, pv: jnp.concatenate([pk, pv], axis=1).reshape(1, -1)
        add = jnp.concatenate([emb(nsa_cmp_pos_k[l][:16], z16), emb(nsa_cmp_pos_k[l][16:], z16),
                               emb(z16, nsa_cmp_pos_v[l][:16]), emb(z16, nsa_cmp_pos_v[l][16:])], axis=0)

        def w1_embed(w1_half, is_v):
            w = w1_half.reshape(NSA_CMP_STRIDE, HEAD_DIM, NSA_CMP_HIDDEN)
            zz = jnp.zeros_like(w)
            parts = (zz, w) if is_v else (w, zz)
            return jnp.concatenate(parts, axis=1).reshape(2 * half, NSA_CMP_HIDDEN)

        w1 = jnp.stack([w1_embed(nsa_cmp_k_w1[l][:half], False), w1_embed(nsa_cmp_k_w1[l][half:], False),
                        w1_embed(nsa_cmp_v_w1[l][:half], True), w1_embed(nsa_cmp_v_w1[l][half:], True)]).astype(BF16)
        w2 = jnp.stack([_pad_cols(nsa_cmp_k_w2[l], HP), _pad_cols(nsa_cmp_v_w2[l], HP)]).astype(BF16)
        kc, vc = _compress(kv_cmp.reshape(B, n_half, NSA_CMP_STRIDE * 2 * HEAD_DIM), add, w1, w2)
        o_cmp, sel = _cmp_attn(b3(nsa_q), kc, vc, ovl_t, n_top)
        o_slc = _slc(b3(nsa_qr), b3(kv_slc), sel, expand)
        o_win = _band(b3(nsa_qr), b3(kv_win), 1, NSA_HEADS, NSA_WINDOW, F32)
        y_ret = _retention(b3(ret_q), b3(ret_k), b3(ret_v), b3(ret_g), dmask, zeta, xi, gch, row(ret_gn_g[l]))
        y_rw = _rwkv(b3(rw), row(_pad_heads(rwkv_r_k[l].reshape(-1))), row(_pad_heads(rwkv_ln_g[l])),
                     row(_pad_heads(rwkv_ln_b[l])))
        y_swa = _band(b3(swa_q), b3(swa_kv), SWA_KV_HEADS, SWA_HEADS // SWA_KV_HEADS, SWA_WINDOW, BF16,
                      sinks=swa_sinks[l])
        gcol = c('nsa_gate').reshape(D, NSA_HEADS, 3)
        g_exp = [jnp.repeat(gcol[:, :, j], HP, axis=1) for j in range(3)]
        w_gate = jnp.concatenate([c('branch_gate')] + g_exp, axis=1).astype(BF16)
        f2 = lambda t: t.reshape(T, t.shape[-1])
        xf = _merge(xf, f2(o_cmp), f2(o_slc), f2(o_win), f2(y_ret), f2(y_rw), f2(y_swa),
                    row(mix_pre_g[l]), row(mix_post_g[l]), w_gate,
                    _pad_heads(w_br_nsa[l], axis=0).astype(BF16), w_br_ret[l].astype(BF16),
                    _pad_heads(w_br_rwkv[l], axis=0).astype(BF16), _pad_heads(w_br_swa[l], axis=0).astype(BF16),
                    w_out[l].astype(BF16))
        xf = _ffn(xf, row(ffn2_pre_g[l]), row(ffn2_post_g[l]), ffn2_w_gate[l].astype(BF16),
                  ffn2_w_up[l].astype(BF16), ffn2_w_down[l].astype(BF16))
    return xf.reshape(B, S, D)
```

```python
import functools
import math

import numpy as np
import jax
import jax.numpy as jnp
from jax import lax
from jax.experimental import pallas as pl
from jax.experimental.pallas import tpu as pltpu

F32 = jnp.float32
BF16 = jnp.bfloat16
HIGHEST = lax.Precision.HIGHEST

D_MODEL = 1024
HEAD_DIM = 64
HP = 128
ROPE_THETA = 10000.0
RMS_EPS = 1e-6
GN_EPS = 1e-5
RWKV_LN_EPS = 64e-5
D_FF = 2816
NEG_INF = -1e30
FORCED_SCORE = 1e9
LOG2E = math.log2(math.e)

NSA_HEADS = 4
NSA_CMP_LEN = 32
NSA_CMP_STRIDE = 16
NSA_CMP_HIDDEN = 256
NSA_SEL_BLOCK = 64
NSA_TOP_N = 16
NSA_WINDOW = 512
RET_HEADS = 4
RET_DV = 128
RET_CHUNK = 128
RWKV_HEADS = 4
RWKV_CHUNK = 64
SWA_HEADS = 4
SWA_KV_HEADS = 2
SWA_WINDOW = 128
ATT_BLOCK = 128
SLC_TILE = 512

VMEM_LIMIT = 56 * 1024 * 1024


def _cp(*sem):
    return pltpu.CompilerParams(dimension_semantics=sem, vmem_limit_bytes=VMEM_LIMIT)


def _const_spec(shape):
    nd = len(shape)
    return pl.BlockSpec(shape, lambda *_: (0,) * nd, pipeline_mode=pl.Buffered(1))


def _dot(a, b):
    return jnp.dot(a.astype(BF16), b.astype(BF16), preferred_element_type=F32)


def _dot_nt(a, b):
    return lax.dot_general(a.astype(BF16), b.astype(BF16), (((1,), (1,)), ((), ())),
                           preferred_element_type=F32)


def _dot_tn(a, b):
    return lax.dot_general(a.astype(BF16), b.astype(BF16), (((0,), (0,)), ((), ())),
                           preferred_element_type=F32)


def _dot_hi(a, b):
    return jnp.dot(a, b, precision=HIGHEST, preferred_element_type=F32)


def _rms(x, g):
    return x * lax.rsqrt(jnp.mean(x * x, axis=-1, keepdims=True) + RMS_EPS) * g


def _rope(t, cos, sin_lo, sin_hi):
    return t * cos + pltpu.roll(t, 32, 1) * sin_hi - pltpu.roll(t, HP - 32, 1) * sin_lo


def _ffn_kernel(x_ref, pre_ref, post_ref, wg_ref, wu_ref, wd_ref, o_ref, *, ck):
    x = x_ref[...]
    h = _rms(x, pre_ref[...]).astype(BF16)
    acc = jnp.zeros(x.shape, F32)
    for c in range(D_FF // ck):
        g = jnp.dot(h, wg_ref[:, c * ck:(c + 1) * ck], preferred_element_type=F32)
        u = jnp.dot(h, wu_ref[:, c * ck:(c + 1) * ck], preferred_element_type=F32)
        a = (jax.nn.silu(g) * u).astype(BF16)
        acc = acc + jnp.dot(a, wd_ref[c * ck:(c + 1) * ck, :], preferred_element_type=F32)
    o_ref[...] = x + 0.5 * _rms(acc, post_ref[...])


def _ffn(x, pre_g, post_g, wg, wu, wd, tm=512):
    T, D = x.shape
    return pl.pallas_call(
        functools.partial(_ffn_kernel, ck=256),
        grid=(T // tm,),
        in_specs=[pl.BlockSpec((tm, D), lambda i: (i, 0)),
                  _const_spec((1, D)), _const_spec((1, D)),
                  _const_spec((D, D_FF)), _const_spec((D, D_FF)), _const_spec((D_FF, D))],
        out_specs=pl.BlockSpec((tm, D), lambda i: (i, 0)),
        out_shape=jax.ShapeDtypeStruct((T, D), F32),
        compiler_params=_cp("arbitrary"),
    )(x, pre_g, post_g, wg, wu, wd)


RW_COLS = 1024
RW_OUT = 7 * 4 * HP


def _inproj_kernel(x_ref, pre_ref, rope_ref, wn_ref, wr_ref, ww_ref, ws_ref,
                   mu_ref, w0_ref, w2_ref, a0_ref, a2_ref, g2_ref, kk_ref, ka_ref,
                   nq_ref, nqr_ref, nslc_ref, nwin_ref, ncmp_ref,
                   rq_ref, rk_ref, rv_ref, rg_ref, rw_ref, sq_ref, skv_ref,
                   carry_ref, *, tiles_per_seq):
    i = pl.program_id(0)
    x = x_ref[...]
    tm = x.shape[0]
    h = _rms(x, pre_ref[...]).astype(BF16)
    lane = lax.broadcasted_iota(jnp.int32, (tm, HP), 1)
    low = lane < HEAD_DIM
    cos = rope_ref[:, 0:HP]
    s_lo = rope_ref[:, HP:2 * HP]
    s_hi = rope_ref[:, 2 * HP:3 * HP]
    rope = lambda t: _rope(t, cos, s_lo, s_hi)
    rope_k = lambda t: _rope(t, jnp.where(low, cos, 1.0), jnp.where(low, s_lo, 0.0), jnp.where(low, s_hi, 0.0))
    pair = lambda t, j: t[:, j * HP:(j + 1) * HP]
    head = lambda t, hd: jnp.where(low, pair(t, hd // 2) if hd % 2 == 0 else pltpu.roll(pair(t, hd // 2), HEAD_DIM, 1), 0.0)

    def spread(ref, t, n_heads, col0=0):
        for hd in range(n_heads):
            ref[:, col0 + hd * HP:col0 + (hd + 1) * HP] = head(t, hd).astype(ref.dtype)

    p = jnp.dot(h, wn_ref[...], preferred_element_type=F32)
    spread(nq_ref, p, 4)
    spread(nqr_ref, jnp.concatenate([rope(pair(p, 0)), rope(pair(p, 1))], axis=1), 4)
    slc = rope_k(pair(p, 2))
    nslc_ref[:, 0:HP] = (jnp.where(low, slc, 0.0) + rope_ref[:, 3 * HP:4 * HP]).astype(BF16)
    nslc_ref[:, HP:2 * HP] = head(slc, 1).astype(BF16)
    spread(nwin_ref, rope_k(pair(p, 3)), 2)
    ncmp_ref[...] = pair(p, 4)

    p = jnp.dot(h, wr_ref[...], preferred_element_type=F32)
    spread(rq_ref, jnp.concatenate([rope(pair(p, 0)), rope(pair(p, 1))], axis=1), 4)
    spread(rk_ref, jnp.concatenate([rope(pair(p, 2)), rope(pair(p, 3))], axis=1), 4)
    rv_ref[...] = p[:, 4 * HP:4 * HP + 512].astype(BF16)
    rg_ref[...] = p[:, 4 * HP + 512:4 * HP + 1024]

    p = jnp.dot(h, ws_ref[...], preferred_element_type=F32)
    spread(sq_ref, jnp.concatenate([rope(pair(p, 0)), rope(pair(p, 1))], axis=1), 4)
    spread(skv_ref, rope(pair(p, 2)), 2)
    spread(skv_ref, pair(p, 3), 2, col0=2 * HP)

    p = jnp.dot(h, ww_ref[...], preferred_element_type=F32)
    @pl.when(i == 0)
    def _():
        carry_ref[...] = jnp.zeros(carry_ref.shape, F32)

    first = jnp.where(i % tiles_per_seq == 0, 0.0, carry_ref[0:1, :])
    row = lax.broadcasted_iota(jnp.int32, p.shape, 0)
    prev = jnp.where(row == 0, first, pltpu.roll(p, 1, 0))
    carry_ref[0:1, :] = p[tm - 1:tm, :]
    xm = p + mu_ref[...] * (prev - p)
    WC = RWKV_HEADS * HEAD_DIM
    r = xm[:, 0:WC]
    k = xm[:, WC:2 * WC]
    v = xm[:, 2 * WC:3 * WC]
    lora = xm[:, 3 * WC:3 * WC + HP]
    gl = xm[:, 3 * WC + HP:3 * WC + 2 * HP]
    z = -(w0_ref[...] + _dot(jnp.tanh(lora), w2_ref[...]))
    softplus = jnp.maximum(z, 0.0) + jnp.log(1.0 + jnp.exp(-jnp.abs(z)))
    logw = -jnp.exp(-softplus - 0.5)
    a = jax.nn.sigmoid(a0_ref[...] + _dot(lora, a2_ref[...]))
    g = _dot(jax.nn.sigmoid(gl), g2_ref[...])
    kkr = k * kk_ref[...]
    k2 = k * (1.0 + (a - 1.0) * ka_ref[...])
    W4 = 4 * HP
    spread(rw_ref, r, 4)
    spread(rw_ref, k2, 4, col0=W4)
    spread(rw_ref, v, 4, col0=2 * W4)
    for hd in range(4):
        kj = head(kkr, hd)
        kkj = kj / jnp.maximum(jnp.sqrt(jnp.sum(kj * kj, axis=-1, keepdims=True)), 1e-12)
        rw_ref[:, 3 * W4 + hd * HP:3 * W4 + (hd + 1) * HP] = -kkj
        rw_ref[:, 4 * W4 + hd * HP:4 * W4 + (hd + 1) * HP] = kkj * head(a, hd)
    spread(rw_ref, logw, 4, col0=5 * W4)
    spread(rw_ref, g, 4, col0=6 * W4)


def _inproj(x, S, pre_g, rope_tab, wn, wr, ww, ws, mu, w0, w2, a0, a2, g2, k_k, k_a, tm=256):
    T, D = x.shape
    row = lambda w: pl.BlockSpec((tm, w), lambda i: (i, 0))
    tps = S // tm
    outs = [4 * HP, 4 * HP, 2 * HP, 2 * HP, HP, 4 * HP, 4 * HP, 512, 512, RW_OUT, 4 * HP, 4 * HP]
    dts = [BF16, BF16, BF16, BF16, F32, F32, F32, BF16, F32, F32, BF16, BF16]
    return pl.pallas_call(
        functools.partial(_inproj_kernel, tiles_per_seq=tps),
        grid=(T // tm,),
        in_specs=[row(D), _const_spec((1, D)),
                  pl.BlockSpec((tm, 4 * HP), lambda i: (i % tps, 0)),
                  _const_spec(wn.shape), _const_spec(wr.shape), _const_spec(ww.shape), _const_spec(ws.shape),
                  _const_spec(mu.shape), _const_spec(w0.shape), _const_spec(w2.shape), _const_spec(a0.shape),
                  _const_spec(a2.shape), _const_spec(g2.shape), _const_spec(k_k.shape), _const_spec(k_a.shape)],
        out_specs=[row(w) for w in outs],
        out_shape=[jax.ShapeDtypeStruct((T, w), dt) for w, dt in zip(outs, dts)],
        scratch_shapes=[pltpu.VMEM((8, RW_COLS), F32)],
        compiler_params=_cp("arbitrary"),
    )(x, pre_g, rope_tab, wn, wr, ww, ws, mu, w0, w2, a0, a2, g2, k_k, k_a)


def _compress_kernel(h_ref, add_ref, w1_ref, w2_ref, kc_ref, vc_ref):
    hb = h_ref[...]
    n_half = hb.shape[0]
    for t, o_ref in ((0, kc_ref), (1, vc_ref)):
        top = _dot(hb + add_ref[2 * t:2 * t + 1, :], w1_ref[2 * t])
        bot = _dot(hb + add_ref[2 * t + 1:2 * t + 2, :], w1_ref[2 * t + 1])
        pre = top + pltpu.roll(bot, n_half - 1, 0)
        o_ref[...] = _dot(jax.nn.gelu(pre), w2_ref[t])


def _compress(hb, add, w1, w2):
    B, n_half, W = hb.shape
    return pl.pallas_call(
        _compress_kernel,
        grid=(B,),
        in_specs=[pl.BlockSpec((None, n_half, W), lambda b: (b, 0, 0)),
                  _const_spec(add.shape), _const_spec(w1.shape), _const_spec(w2.shape)],
        out_specs=[pl.BlockSpec((None, n_half, HP), lambda b: (b, 0, 0))] * 2,
        out_shape=[jax.ShapeDtypeStruct((B, n_half, HP), F32)] * 2,
        compiler_params=_cp("arbitrary"),
    )(hb, add, w1, w2)


def _cmp_kernel(q_ref, kc_ref, vc_ref, ovl_ref, o_ref, pen_ref, *, n_top):
    i = pl.program_id(1)
    tq = q_ref.shape[0]
    n_half = kc_ref.shape[0]
    n_sel = ovl_ref.shape[0]
    kc = kc_ref[...]
    vc = vc_ref[...]
    tpos = i * tq + lax.broadcasted_iota(jnp.int32, (tq, n_half), 0)
    n_id = lax.broadcasted_iota(jnp.int32, (tq, n_half), 1)
    mask = (n_id * NSA_CMP_STRIDE + NSA_CMP_LEN - 1 <= tpos) & (n_id < n_half - 1)
    maskf = mask.astype(F32)
    heads = range(NSA_HEADS)
    s = [jnp.where(mask, _dot_nt(q_ref[:, hd * HP:(hd + 1) * HP], kc), NEG_INF) for hd in heads]
    m = [jnp.max(s[hd], axis=-1, keepdims=True) for hd in heads]
    p = [jnp.exp2(s[hd] - m[hd]) * maskf for hd in heads]
    p = [p[hd] / jnp.maximum(jnp.sum(p[hd], axis=-1, keepdims=True), 1e-30) for hd in heads]
    o = [_dot(p[hd], vc) for hd in heads]
    imp = [_dot_nt(ovl_ref[...], p[hd]) for hd in heads]
    imp_t = (imp[0] + imp[1]) + (imp[2] + imp[3])
    for hd in heads:
        o_ref[:, hd * HP:(hd + 1) * HP] = o[hd]
    blk = lax.broadcasted_iota(jnp.int32, (n_sel, tq), 0)
    cur = (i * tq + lax.broadcasted_iota(jnp.int32, (n_sel, tq), 1)) // NSA_SEL_BLOCK
    forced = (blk == 0) | (blk == cur) | (blk == cur - 1)
    valid = blk <= cur
    score = jnp.where(forced, FORCED_SCORE, jnp.where(valid, imp_t, -FORCED_SCORE))
    rank = jnp.zeros((n_sel, tq), F32)
    for b in range(n_sel):
        sb = score[b:b + 1, :]
        before = (sb > score) | ((sb == score) & (blk > b))
        rank = rank + before.astype(F32)
    pen_t = jnp.where((rank < n_top) & valid, 0.0, NEG_INF)
    pieces = [jnp.zeros((HEAD_DIM, tq), F32), pen_t]
    if HP - HEAD_DIM - n_sel:
        pieces.append(jnp.zeros((HP - HEAD_DIM - n_sel, tq), F32))
    pen_ref[...] = jnp.concatenate(pieces, axis=0).T.astype(BF16)


def _cmp_attn(q, kc, vc, ovl_t, n_top, tq=2 * ATT_BLOCK):
    B, S, W = q.shape
    n_half = kc.shape[1]
    n_sel = ovl_t.shape[0]
    return pl.pallas_call(
        functools.partial(_cmp_kernel, n_top=n_top),
        grid=(B, S // tq),
        in_specs=[pl.BlockSpec((None, tq, W), lambda b, i: (b, i, 0)),
                  pl.BlockSpec((None, n_half, HP), lambda b, i: (b, 0, 0)),
                  pl.BlockSpec((None, n_half, HP), lambda b, i: (b, 0, 0)),
                  _const_spec(ovl_t.shape)],
        out_specs=[pl.BlockSpec((None, tq, W), lambda b, i: (b, i, 0)),
                   pl.BlockSpec((None, tq, HP), lambda b, i: (b, i, 0))],
        out_shape=[jax.ShapeDtypeStruct((B, S, W), F32), jax.ShapeDtypeStruct((B, S, HP), BF16)],
        compiler_params=_cp("arbitrary", "arbitrary"),
    )(q, kc, vc, ovl_t)


def _band_kernel(*refs, n_kv, group, window, back, use_sink):
    if use_sink:
        q_ref, kv_ref, sink_ref, o_ref = refs
    else:
        q_ref, kv_ref, o_ref = refs
    i = pl.program_id(1)
    tq = q_ref.shape[0]
    span = back + tq
    start = pl.multiple_of(jnp.maximum(i * tq - back, 0), ATT_BLOCK)
    tpos = i * tq + lax.broadcasted_iota(jnp.int32, (tq, span), 0)
    diff = tpos - (start + lax.broadcasted_iota(jnp.int32, (tq, span), 1))
    mask = (diff >= 0) & (diff < window)
    heads = range(n_kv * group)
    hsl = lambda hd: slice(hd * HP, (hd + 1) * HP)
    k = [kv_ref[pl.ds(start, span), hsl(kh)] for kh in range(n_kv)]
    v = [kv_ref[pl.ds(start, span), hsl(n_kv + kh)] for kh in range(n_kv)]
    s = [jnp.where(mask, _dot_nt(q_ref[:, hsl(hd)], k[hd // group]), NEG_INF) for hd in heads]
    m = [jnp.max(s[hd], axis=-1, keepdims=True) for hd in heads]
    if use_sink:
        m = [jnp.maximum(m[hd], sink_ref[hd]) for hd in heads]
    p = [jnp.exp2(s[hd] - m[hd]) for hd in heads]
    l = [jnp.sum(p[hd], axis=-1, keepdims=True) for hd in heads]
    if use_sink:
        l = [l[hd] + jnp.exp2(sink_ref[hd] - m[hd]) for hd in heads]
    o = [_dot(p[hd], v[hd // group]) for hd in heads]
    for hd in heads:
        o_ref[:, hsl(hd)] = (o[hd] / l[hd]).astype(o_ref.dtype)


def _band(q, kv, n_kv, group, window, out_dtype, sinks=None, tq=2 * ATT_BLOCK):
    B, S, W = q.shape
    back = -(-(window - 1) // ATT_BLOCK) * ATT_BLOCK
    assert S >= back + tq
    use_sink = sinks is not None
    args = [q, kv]
    in_specs = [pl.BlockSpec((None, tq, W), lambda b, i: (b, i, 0)),
                pl.BlockSpec((None, S, kv.shape[2]), lambda b, i: (b, 0, 0))]
    if use_sink:
        args.append(sinks)
        in_specs.append(pl.BlockSpec(memory_space=pltpu.SMEM))
    return pl.pallas_call(
        functools.partial(_band_kernel, n_kv=n_kv, group=group, window=window, back=back, use_sink=use_sink),
        grid=(B, S // tq),
        in_specs=in_specs,
        out_specs=pl.BlockSpec((None, tq, W), lambda b, i: (b, i, 0)),
        out_shape=jax.ShapeDtypeStruct((B, S, W), out_dtype),
        compiler_params=_cp("arbitrary", "arbitrary"),
    )(*args)


def _slc_kernel(q_ref, kv_ref, pen_ref, o_ref, *, tk):
    i = pl.program_id(1)
    tq = q_ref.shape[0]
    n_t = (i * tq + tq - 1) // tk + 1
    heads = range(NSA_HEADS)
    pen = pen_ref[...].astype(F32)
    q = [(q_ref[:, hd * HP:(hd + 1) * HP].astype(F32) + pen).astype(BF16) for hd in heads]

    def step(j, carry, causal):
        r0 = pl.multiple_of(j * tk, tk)
        k_t = kv_ref[pl.ds(r0, tk), 0:HP]
        v_t = kv_ref[pl.ds(r0, tk), HP:2 * HP]
        s = [_dot_nt(q[hd], k_t) for hd in heads]
        if causal:
            tpos = i * tq + lax.broadcasted_iota(jnp.int32, (tq, tk), 0)
            ok = tpos >= j * tk + lax.broadcasted_iota(jnp.int32, (tq, tk), 1)
            s = [jnp.where(ok, s[hd], NEG_INF) for hd in heads]
        m_new = [jnp.maximum(carry[hd][0], jnp.max(s[hd], axis=-1, keepdims=True)) for hd in heads]
        p = [jnp.exp2(s[hd] - m_new[hd]) for hd in heads]
        pv = [_dot(p[hd], v_t) for hd in heads]
        out = []
        for hd in heads:
            m, l, acc = carry[hd]
            alpha = jnp.exp2(m - m_new[hd])
            out.append((m_new[hd], l * alpha + jnp.sum(p[hd], axis=-1, keepdims=True), acc * alpha + pv[hd]))
        return tuple(out)

    init = tuple((jnp.full((tq, 1), NEG_INF, F32), jnp.zeros((tq, 1), F32), jnp.zeros((tq, HP), F32))
                 for _ in heads)
    mid = lax.fori_loop(0, n_t - 1, lambda j, c: step(j, c, False), init)
    fin = step(n_t - 1, mid, True)
    for hd in heads:
        o_ref[:, hd * HP:(hd + 1) * HP] = fin[hd][2] / fin[hd][1]


def _slc(q, kv, pen, tq=ATT_BLOCK, tk=SLC_TILE):
    B, S, W = q.shape
    assert tk % tq == 0 and S % tk == 0
    return pl.pallas_call(
        functools.partial(_slc_kernel, tk=tk),
        grid=(B, S // tq),
        in_specs=[pl.BlockSpec((None, tq, W), lambda b, i: (b, i, 0)),
                  pl.BlockSpec((None, S, kv.shape[2]), lambda b, i: (b, 0, 0)),
                  pl.BlockSpec((None, tq, HP), lambda b, i: (b, i, 0))],
        out_specs=pl.BlockSpec((None, tq, W), lambda b, i: (b, i, 0)),
        out_shape=jax.ShapeDtypeStruct((B, S, W), F32),
        compiler_params=_cp("arbitrary", "arbitrary"),
    )(q, kv, pen)


def _ret_kernel(q_ref, k_ref, v_ref, g_ref, dm_ref, zeta_ref, xi_ref, gch_ref, gn_ref, o_ref, st_ref, *, n_chunks):
    c = pl.program_id(1)

    @pl.when(c == 0)
    def _():
        st_ref[...] = jnp.zeros(st_ref.shape, F32)

    C = RET_CHUNK
    pairs = [(ch, hd) for ch in range(n_chunks) for hd in range(RET_HEADS)]
    rows = lambda ch: slice(ch * C, (ch + 1) * C)
    q = {p: q_ref[rows(p[0]), p[1] * HP:(p[1] + 1) * HP] for p in pairs}
    k = {p: k_ref[rows(p[0]), p[1] * HP:(p[1] + 1) * HP] for p in pairs}
    v = {p: v_ref[rows(p[0]), p[1] * RET_DV:(p[1] + 1) * RET_DV] for p in pairs}
    inner = {p: _dot_nt(q[p], k[p]) * dm_ref[p[1]] for p in pairs}
    o = {p: _dot(inner[p], v[p]) for p in pairs}
    kv = {p: _dot_tn(k[p] * zeta_ref[p[1]], v[p]) for p in pairs}
    for hd in range(RET_HEADS):
        state = st_ref[hd]
        for ch in range(n_chunks):
            p = (ch, hd)
            o[p] = o[p] + _dot(q[p] * xi_ref[hd], state)
            state = state * gch_ref[hd] + kv[p]
        st_ref[hd] = state
    for p in pairs:
        ch, hd = p
        mu = jnp.mean(o[p], axis=-1, keepdims=True)
        d = o[p] - mu
        var = jnp.mean(d * d, axis=-1, keepdims=True)
        on = d * lax.rsqrt(var + GN_EPS) * gn_ref[:, hd * RET_DV:(hd + 1) * RET_DV]
        gate = jax.nn.silu(g_ref[rows(ch), hd * RET_DV:(hd + 1) * RET_DV])
        o_ref[rows(ch), hd * RET_DV:(hd + 1) * RET_DV] = (gate * on).astype(BF16)


def _retention(q, k, v, g, dmask, zeta, xi, gch, gn_g, n_chunks=2):
    B, S, _ = q.shape
    C = n_chunks * RET_CHUNK
    blk = lambda w: pl.BlockSpec((None, C, w), lambda b, c: (b, c, 0))
    return pl.pallas_call(
        functools.partial(_ret_kernel, n_chunks=n_chunks),
        grid=(B, S // C),
        in_specs=[blk(4 * HP), blk(4 * HP), blk(512), blk(512),
                  _const_spec(dmask.shape), _const_spec(zeta.shape), _const_spec(xi.shape),
                  _const_spec(gch.shape), _const_spec(gn_g.shape)],
        out_specs=blk(512),
        out_shape=jax.ShapeDtypeStruct((B, S, 512), BF16),
        scratch_shapes=[pltpu.VMEM((RET_HEADS, HP, RET_DV), F32)],
        compiler_params=_cp("arbitrary", "arbitrary"),
    )(q, k, v, g, dmask, zeta, xi, gch, gn_g)


def _rwkv_kernel(rw_ref, rk_ref, lng_ref, lnb_ref, o_ref, z_ref, *, n_chunks):
    c = pl.program_id(1)

    @pl.when(c == 0)
    def _():
        z_ref[...] = jnp.zeros(z_ref.shape, F32)

    L = RWKV_CHUNK
    W4 = 4 * HP
    ri = lax.broadcasted_iota(jnp.int32, (2 * L, 2 * L), 0)
    ci = lax.broadcasted_iota(jnp.int32, (2 * L, 2 * L), 1) % L
    a_mask = ((ri < L) & (ri > ci)) | (ri - L >= ci)
    tri = (lax.broadcasted_iota(jnp.int32, (L, L), 0) >= lax.broadcasted_iota(jnp.int32, (L, L), 1)).astype(BF16)
    dr = lax.broadcasted_iota(jnp.int32, (HP, HP), 0)
    dc = lax.broadcasted_iota(jnp.int32, (HP, HP), 1)
    real = (lax.broadcasted_iota(jnp.int32, (L, HP), 1) < HEAD_DIM).astype(F32)
    zeros = jnp.zeros((L, HP), F32)
    tdot = lambda t: jnp.dot(tri, t, preferred_element_type=F32)
    pairs = [(ch, hd) for ch in range(n_chunks) for hd in range(RWKV_HEADS)]
    col = lambda ch, hd, gi: rw_ref[ch * L:(ch + 1) * L, gi * W4 + hd * HP:gi * W4 + (hd + 1) * HP]
    cum_all = []
    for ch in range(n_chunks):
        lw_all = rw_ref[ch * L:(ch + 1) * L, 5 * W4:6 * W4]
        hi = lw_all.astype(BF16)
        rem = lw_all - hi.astype(F32)
        mid = rem.astype(BF16)
        lo = (rem - mid.astype(F32)).astype(BF16)
        cum_all.append(tdot(hi) + tdot(mid) + tdot(lo))
    cum = {p: cum_all[p[0]][:, p[1] * HP:(p[1] + 1) * HP] for p in pairs}
    r, k, v, a, b, lw = ({p: col(p[0], p[1], gi) for p in pairs} for gi in range(6))
    e_inv = {p: jnp.exp(-cum[p]) for p in pairs}
    at = {p: a[p] * jnp.exp(cum[p] - lw[p]) for p in pairs}
    rt = {p: r[p] * jnp.exp(cum[p]) for p in pairs}
    amat = {p: jnp.where(a_mask, _dot_nt(jnp.concatenate([at[p], rt[p]], axis=0),
                                         jnp.concatenate([b[p] * e_inv[p], k[p] * e_inv[p]], axis=0)), 0.0)
            for p in pairs}
    half = lambda t: pltpu.roll(t, HEAD_DIM, 1)
    x = {p: at[p] + half(_dot(amat[p][0:L], jnp.concatenate([zeros, v[p]], axis=0))) for p in pairs}
    npow = {p: amat[p][0:L, 0:L] for p in pairs}
    for s in range(int(math.log2(L))):
        x = {p: x[p] + _dot(npow[p], x[p]) for p in pairs}
        if (2 << s) < L:
            npow = {p: _dot(npow[p], npow[p]) for p in pairs}
    rhs = {p: jnp.concatenate([x[p], half(v[p])], axis=0) for p in pairs}
    yq = {p: _dot(amat[p][L:2 * L], rhs[p]) for p in pairs}
    mn = {}
    for p in pairs:
        e_rem = jnp.exp(cum[p][L - 1:L, :] - cum[p])
        mn[p] = _dot_tn(jnp.concatenate([b[p] * e_rem, k[p] * e_rem], axis=0), rhs[p])
    low = (dc < HEAD_DIM).astype(F32)
    y = {}
    for hd in range(RWKV_HEADS):
        z = z_ref[hd]
        for ch in range(n_chunks):
            p = (ch, hd)
            y[p] = half(yq[p]) * real + _dot(rt[p] + yq[p] * real, z)
            decay = jnp.where(dr == dc, jnp.exp(cum[p][L - 1:L, :]), 0.0)
            z = _dot(decay + mn[p] * low, z) + half(mn[p]) * low
        z_ref[hd] = z
    for p in pairs:
        ch, hd = p
        mu = jnp.sum(y[p], axis=-1, keepdims=True) * (1.0 / HEAD_DIM)
        d = (y[p] - mu) * real
        var = jnp.sum(d * d, axis=-1, keepdims=True) * (1.0 / HEAD_DIM)
        yn = d * lax.rsqrt(var + RWKV_LN_EPS) * lng_ref[:, hd * HP:(hd + 1) * HP] + lnb_ref[:, hd * HP:(hd + 1) * HP]
        bonus = jnp.sum(r[p] * k[p] * rk_ref[:, hd * HP:(hd + 1) * HP], axis=-1, keepdims=True) * v[p]
        o_ref[ch * L:(ch + 1) * L, hd * HP:(hd + 1) * HP] = ((yn + bonus) * col(ch, hd, 6)).astype(o_ref.dtype)


def _rwkv(rw, r_k, ln_g, ln_b, n_chunks=4):
    B, S, _ = rw.shape
    L = n_chunks * RWKV_CHUNK
    return pl.pallas_call(
        functools.partial(_rwkv_kernel, n_chunks=n_chunks),
        grid=(B, S // L),
        in_specs=[pl.BlockSpec((None, L, RW_OUT), lambda b, c: (b, c, 0)),
                  _const_spec(r_k.shape), _const_spec(ln_g.shape), _const_spec(ln_b.shape)],
        out_specs=pl.BlockSpec((None, L, 4 * HP), lambda b, c: (b, c, 0)),
        out_shape=jax.ShapeDtypeStruct((B, S, 4 * HP), BF16),
        scratch_shapes=[pltpu.VMEM((RWKV_HEADS, HP, HP), F32)],
        compiler_params=_cp("arbitrary", "arbitrary"),
    )(rw, r_k, ln_g, ln_b)


def _merge_kernel(x_ref, ocmp_ref, oslc_ref, owin_ref, yret_ref, yrw_ref, yswa_ref,
                  pre_ref, post_ref, wg_ref, wn_ref, wr_ref, ww_ref, ws_ref, wo_ref, o_ref):
    x = x_ref[...]
    D = x.shape[1]
    W4 = 4 * HP
    h = _rms(x, pre_ref[...]).astype(BF16)
    tm = x.shape[0]
    gate = lambda j, w: jax.nn.sigmoid(jnp.dot(h, wg_ref[:, j:j + w], preferred_element_type=F32))
    g_nsa = gate(4 * D, HP)
    bc = lambda j: jnp.broadcast_to(g_nsa[:, j:j + 1], (tm, HP))
    y_nsa = jnp.concatenate(
        [bc(3 * hd) * ocmp_ref[:, hd * HP:(hd + 1) * HP] + bc(3 * hd + 1) * oslc_ref[:, hd * HP:(hd + 1) * HP]
         + bc(3 * hd + 2) * owin_ref[:, hd * HP:(hd + 1) * HP] for hd in range(NSA_HEADS)], axis=1)
    merged = gate(0, D) * _dot(y_nsa, wn_ref[...])
    merged = merged + gate(D, D) * _dot(yret_ref[...], wr_ref[...])
    merged = merged + gate(2 * D, D) * _dot(yrw_ref[...], ww_ref[...])
    merged = merged + gate(3 * D, D) * _dot(yswa_ref[...], ws_ref[...])
    o_ref[...] = x + _rms(_dot(merged, wo_ref[...]), post_ref[...])


def _merge(x, ocmp, oslc, owin, yret, yrw, yswa, pre_g, post_g, wg, wn, wr, ww, ws, wo, tm=256):
    T, D = x.shape
    row = lambda w: pl.BlockSpec((tm, w), lambda i: (i, 0))
    return pl.pallas_call(
        _merge_kernel,
        grid=(T // tm,),
        in_specs=[row(D), row(4 * HP), row(4 * HP), row(4 * HP), row(512), row(4 * HP), row(4 * HP),
                  _const_spec((1, D)), _const_spec((1, D)), _const_spec(wg.shape), _const_spec(wn.shape),
                  _const_spec(wr.shape), _const_spec(ww.shape), _const_spec(ws.shape), _const_spec(wo.shape)],
        out_specs=row(D),
        out_shape=jax.ShapeDtypeStruct((T, D), F32),
        compiler_params=_cp("arbitrary"),
    )(x, ocmp, oslc, owin, yret, yrw, yswa, pre_g, post_g, wg, wn, wr, ww, ws, wo)


def _column_layout():
    spec = (
        ('nsa_q', 256), ('nsa_k_cmp', 64), ('nsa_v_cmp', 64), ('nsa_k_slc', 64), ('nsa_v_slc', 64),
        ('nsa_k_win', 64), ('nsa_v_win', 64), ('nsa_gate', 12),
        ('ret_q', 256), ('ret_k', 256), ('ret_v', 512), ('ret_g', 512),
        ('rwkv', 1024),
        ('swa_q', 256), ('swa_k', 128), ('swa_v', 128),
        ('branch_gate', 4 * D_MODEL),
    )
    layout, start = {}, 0
    for name, width in spec:
        layout[name] = (start, start + width)
        start += width
    return layout


def _pad_heads(w, axis=-1):
    w = jnp.moveaxis(w, axis, -1)
    lead = w.shape[:-1]
    n = w.shape[-1] // HEAD_DIM
    w = w.reshape(lead + (n, HEAD_DIM))
    w = jnp.pad(w, [(0, 0)] * len(lead) + [(0, 0), (0, HP - HEAD_DIM)])
    return jnp.moveaxis(w.reshape(lead + (n * HP,)), -1, axis)


def _pad_cols(w, width):
    return jnp.pad(w, [(0, 0)] * (w.ndim - 1) + [(0, width - w.shape[-1])])


def _rope_table(S):
    half = HEAD_DIM // 2
    inv_freq = jnp.power(ROPE_THETA, -jnp.arange(half, dtype=F32) * 2.0 / HEAD_DIM)
    ang = jnp.arange(S, dtype=jnp.int32).astype(F32)[:, None] * inv_freq[None, :]
    cos, sin, z = jnp.cos(ang), jnp.sin(ang), jnp.zeros((S, half), F32)
    assert S // NSA_SEL_BLOCK <= HP - HEAD_DIM
    blk = jnp.asarray(np.arange(S)[:, None] // NSA_SEL_BLOCK == np.arange(HP)[None, :] - HEAD_DIM, F32)
    return jnp.concatenate([cos, cos, cos, cos, sin, z, sin, z, z, sin, z, sin, blk], axis=1)


def _retention_tables():
    H, C = RET_HEADS, RET_CHUNK
    log_g = jnp.log(1.0 - jnp.power(2.0, -5.0 - jnp.arange(H, dtype=F32)))
    i = jnp.arange(C, dtype=F32)
    diff = i[:, None] - i[None, :]
    dmask = jnp.where(diff >= 0, jnp.exp(log_g[:, None, None] * jnp.maximum(diff, 0.0)), 0.0)
    zeta = jnp.exp(log_g[:, None] * (C - 1 - i)[None, :])
    xi = jnp.exp(log_g[:, None] * (i + 1.0)[None, :])
    gch = jnp.exp(log_g * C)
    bc = lambda t: jnp.broadcast_to(t[:, :, None], (H, C, HP))
    return dmask, bc(zeta), bc(xi), jnp.broadcast_to(gch[:, None, None], (H, HP, RET_DV))


def kernel(x, ffn1_pre_g, ffn1_post_g, ffn1_w_gate, ffn1_w_up, ffn1_w_down, mix_pre_g, mix_post_g, w_in, nsa_cmp_pos_k, nsa_cmp_pos_v, nsa_cmp_k_w1, nsa_cmp_k_w2, nsa_cmp_v_w1, nsa_cmp_v_w2, ret_gn_g, rwkv_mu, rwkv_w0, rwkv_w2, rwkv_a0, rwkv_a2, rwkv_g2, rwkv_k_k, rwkv_k_a, rwkv_r_k, rwkv_ln_g, rwkv_ln_b, swa_sinks, w_br_nsa, w_br_ret, w_br_rwkv, w_br_swa, w_out, ffn2_pre_g, ffn2_post_g, ffn2_w_gate, ffn2_w_up, ffn2_w_down):
    B, S, D = x.shape
    T = B * S
    depth = w_in.shape[0]
    lay = _column_layout()
    n_half = S // NSA_CMP_STRIDE
    n_sel = S // NSA_SEL_BLOCK
    n_top = min(NSA_TOP_N, n_sel)
    scale = HEAD_DIM ** -0.5
    sm_scale = scale * LOG2E
    rope_tab = _rope_table(S)
    dmask, zeta, xi, gch = _retention_tables()
    cs = np.arange(n_half) * NSA_CMP_STRIDE
    ss = np.arange(n_sel) * NSA_SEL_BLOCK
    ovl = ((cs[None, :] <= ss[:, None] + NSA_SEL_BLOCK - 1) & (cs[None, :] + NSA_CMP_LEN - 1 >= ss[:, None])
           & (np.arange(n_half)[None, :] < n_half - 1))
    ovl_t = jnp.asarray(ovl, BF16)
    row = lambda t: t.reshape(1, -1)

    xf = x.reshape(T, D)
    for l in range(depth):
        wl = w_in[l]
        c = lambda name: wl[:, lay[name][0]:lay[name][1]]
        xf = _ffn(xf, row(ffn1_pre_g[l]), row(ffn1_post_g[l]), ffn1_w_gate[l].astype(BF16),
                  ffn1_w_up[l].astype(BF16), ffn1_w_down[l].astype(BF16))
        w_nsa = jnp.concatenate([c('nsa_q') * sm_scale, c('nsa_k_slc'), c('nsa_v_slc'), c('nsa_k_win'),
                                 c('nsa_v_win'), c('nsa_k_cmp'), c('nsa_v_cmp')], axis=1).astype(BF16)
        w_ret = jnp.concatenate([c('ret_q'), c('ret_k') * scale, c('ret_v'), c('ret_g')], axis=1).astype(BF16)
        w_rw = c('rwkv').astype(BF16)
        w_swa = jnp.concatenate([c('swa_q') * sm_scale, c('swa_k'), c('swa_v')], axis=1).astype(BF16)
        w2_p = jnp.pad(rwkv_w2[l], ((0, HP - rwkv_w2.shape[1]), (0, 0)))
        a2_p = jnp.pad(rwkv_a2[l], ((HP - rwkv_a2.shape[1], 0), (0, 0)))
        (nsa_q, nsa_qr, kv_slc, kv_win, kv_cmp, ret_q, ret_k, ret_v, ret_g, rw, swa_q, swa_kv) = _inproj(
            xf, S, row(mix_pre_g[l]), rope_tab, w_nsa, w_ret, w_rw, w_swa, row(rwkv_mu[l]),
            row(rwkv_w0[l]), w2_p, row(rwkv_a0[l]), a2_p, rwkv_g2[l], row(rwkv_k_k[l]), row(rwkv_k_a[l]))
        b3 = lambda t: t.reshape(B, S, t.shape[-1])

        half = NSA_CMP_STRIDE * HEAD_DIM
        z16 = jnp.zeros((NSA_CMP_STRIDE, HEAD_DIM), F32)
        emb = lambda pk, pv: jnp.concatenate([pk, pv], axis=1).reshape(1, -1)
        add = jnp.concatenate([emb(nsa_cmp_pos_k[l][:16], z16), emb(nsa_cmp_pos_k[l][16:], z16),
                               emb(z16, nsa_cmp_pos_v[l][:16]), emb(z16, nsa_cmp_pos_v[l][16:])], axis=0)

        def w1_embed(w1_half, is_v):
            w = w1_half.reshape(NSA_CMP_STRIDE, HEAD_DIM, NSA_CMP_HIDDEN)
            zz = jnp.zeros_like(w)
            parts = (zz, w) if is_v else (w, zz)
            return jnp.concatenate(parts, axis=1).reshape(2 * half, NSA_CMP_HIDDEN)

        w1 = jnp.stack([w1_embed(nsa_cmp_k_w1[l][:half], False), w1_embed(nsa_cmp_k_w1[l][half:], False),
                        w1_embed(nsa_cmp_v_w1[l][:half], True), w1_embed(nsa_cmp_v_w1[l][half:], True)]).astype(BF16)
        w2 = jnp.stack([_pad_cols(nsa_cmp_k_w2[l], HP), _pad_cols(nsa_cmp_v_w2[l], HP)]).astype(BF16)
        kc, vc = _compress(kv_cmp.reshape(B, n_half, NSA_CMP_STRIDE * 2 * HEAD_DIM), add, w1, w2)
        o_cmp, pen = _cmp_attn(b3(nsa_q), kc, vc, ovl_t, n_top)
        o_slc = _slc(b3(nsa_qr), b3(kv_slc), pen)
        o_win = _band(b3(nsa_qr), b3(kv_win), 1, NSA_HEADS, NSA_WINDOW, F32)
        y_ret = _retention(b3(ret_q), b3(ret_k), b3(ret_v), b3(ret_g), dmask, zeta, xi, gch, row(ret_gn_g[l]))
        y_rw = _rwkv(b3(rw), row(_pad_heads(rwkv_r_k[l].reshape(-1))), row(_pad_heads(rwkv_ln_g[l])),
                     row(_pad_heads(rwkv_ln_b[l])))
        y_swa = _band(b3(swa_q), b3(swa_kv), SWA_KV_HEADS, SWA_HEADS // SWA_KV_HEADS, SWA_WINDOW, BF16,
                      sinks=swa_sinks[l] * LOG2E)
        w_gate = jnp.concatenate([c('branch_gate'), _pad_cols(c('nsa_gate'), HP)], axis=1).astype(BF16)
        f2 = lambda t: t.reshape(T, t.shape[-1])
        xf = _merge(xf, f2(o_cmp), f2(o_slc), f2(o_win), f2(y_ret), f2(y_rw), f2(y_swa),
                    row(mix_pre_g[l]), row(mix_post_g[l]), w_gate,
                    _pad_heads(w_br_nsa[l], axis=0).astype(BF16), w_br_ret[l].astype(BF16),
                    _pad_heads(w_br_rwkv[l], axis=0).astype(BF16), _pad_heads(w_br_swa[l], axis=0).astype(BF16),
                    w_out[l].astype(BF16))
        xf = _ffn(xf, row(ffn2_pre_g[l]), row(ffn2_post_g[l]), ffn2_w_gate[l].astype(BF16),
                  ffn2_w_up[l].astype(BF16), ffn2_w_down[l].astype(BF16))
    return xf.reshape(B, S, D)
```

```python
import functools
import math

import numpy as np
import jax
import jax.numpy as jnp
from jax import lax
from jax.experimental import pallas as pl
from jax.experimental.pallas import tpu as pltpu

F32 = jnp.float32
BF16 = jnp.bfloat16
HIGHEST = lax.Precision.HIGHEST

D_MODEL = 1024
HEAD_DIM = 64
HP = 128
ROPE_THETA = 10000.0
RMS_EPS = 1e-6
GN_EPS = 1e-5
RWKV_LN_EPS = 64e-5
D_FF = 2816
NEG_INF = -1e30
FORCED_SCORE = 1e9
LOG2E = math.log2(math.e)

NSA_HEADS = 4
NSA_CMP_LEN = 32
NSA_CMP_STRIDE = 16
NSA_CMP_HIDDEN = 256
NSA_SEL_BLOCK = 64
NSA_TOP_N = 16
NSA_WINDOW = 512
RET_HEADS = 4
RET_DV = 128
RET_CHUNK = 128
RWKV_HEADS = 4
RWKV_CHUNK = 64
SWA_HEADS = 4
SWA_KV_HEADS = 2
SWA_WINDOW = 128
ATT_BLOCK = 128
SLC_TILE = 1024

VMEM_LIMIT = 56 * 1024 * 1024


def _cp(*sem):
    return pltpu.CompilerParams(dimension_semantics=sem, vmem_limit_bytes=VMEM_LIMIT)


def _const_spec(shape):
    nd = len(shape)
    return pl.BlockSpec(shape, lambda *_: (0,) * nd, pipeline_mode=pl.Buffered(1))


def _dot(a, b):
    return jnp.dot(a.astype(BF16), b.astype(BF16), preferred_element_type=F32)


def _dot_nt(a, b):
    return lax.dot_general(a.astype(BF16), b.astype(BF16), (((1,), (1,)), ((), ())),
                           preferred_element_type=F32)


def _dot_tn(a, b):
    return lax.dot_general(a.astype(BF16), b.astype(BF16), (((0,), (0,)), ((), ())),
                           preferred_element_type=F32)


def _dot_hi(a, b):
    return jnp.dot(a, b, precision=HIGHEST, preferred_element_type=F32)


def _rms(x, g):
    return x * lax.rsqrt(jnp.mean(x * x, axis=-1, keepdims=True) + RMS_EPS) * g


def _rope(t, cos, sin_lo, sin_hi):
    return t * cos + pltpu.roll(t, 32, 1) * sin_hi - pltpu.roll(t, HP - 32, 1) * sin_lo


def _ffn_kernel(x_ref, pre_ref, post_ref, wg_ref, wu_ref, wd_ref, o_ref, *, ck):
    x = x_ref[...]
    h = _rms(x, pre_ref[...]).astype(BF16)
    acc = jnp.zeros(x.shape, F32)
    for c in range(D_FF // ck):
        g = jnp.dot(h, wg_ref[:, c * ck:(c + 1) * ck], preferred_element_type=F32)
        u = jnp.dot(h, wu_ref[:, c * ck:(c + 1) * ck], preferred_element_type=F32)
        a = (jax.nn.silu(g) * u).astype(BF16)
        acc = acc + jnp.dot(a, wd_ref[c * ck:(c + 1) * ck, :], preferred_element_type=F32)
    o_ref[...] = x + 0.5 * _rms(acc, post_ref[...])


def _ffn(x, pre_g, post_g, wg, wu, wd, tm=512):
    T, D = x.shape
    return pl.pallas_call(
        functools.partial(_ffn_kernel, ck=256),
        grid=(T // tm,),
        in_specs=[pl.BlockSpec((tm, D), lambda i: (i, 0)),
                  _const_spec((1, D)), _const_spec((1, D)),
                  _const_spec((D, D_FF)), _const_spec((D, D_FF)), _const_spec((D_FF, D))],
        out_specs=pl.BlockSpec((tm, D), lambda i: (i, 0)),
        out_shape=jax.ShapeDtypeStruct((T, D), F32),
        compiler_params=_cp("arbitrary"),
    )(x, pre_g, post_g, wg, wu, wd)


RW_COLS = 1024
RW_OUT = 7 * 4 * HP


def _inproj_kernel(x_ref, pre_ref, rope_ref, wn_ref, wr_ref, ww_ref, ws_ref,
                   mu_ref, w0_ref, w2_ref, a0_ref, a2_ref, g2_ref, kk_ref, ka_ref,
                   nq_ref, nqr_ref, nslc_ref, nwin_ref, ncmp_ref,
                   rq_ref, rk_ref, rv_ref, rg_ref, rw_ref, sq_ref, skv_ref,
                   carry_ref, *, tiles_per_seq):
    i = pl.program_id(0)
    x = x_ref[...]
    tm = x.shape[0]
    h = _rms(x, pre_ref[...]).astype(BF16)
    lane = lax.broadcasted_iota(jnp.int32, (tm, HP), 1)
    low = lane < HEAD_DIM
    cos = rope_ref[:, 0:HP]
    s_lo = rope_ref[:, HP:2 * HP]
    s_hi = rope_ref[:, 2 * HP:3 * HP]
    rope = lambda t: _rope(t, cos, s_lo, s_hi)
    rope_k = lambda t: _rope(t, jnp.where(low, cos, 1.0), jnp.where(low, s_lo, 0.0), jnp.where(low, s_hi, 0.0))
    pair = lambda t, j: t[:, j * HP:(j + 1) * HP]
    head = lambda t, hd: jnp.where(low, pair(t, hd // 2) if hd % 2 == 0 else pltpu.roll(pair(t, hd // 2), HEAD_DIM, 1), 0.0)

    def spread(ref, t, n_heads, col0=0):
        for hd in range(n_heads):
            ref[:, col0 + hd * HP:col0 + (hd + 1) * HP] = head(t, hd).astype(ref.dtype)

    p = jnp.dot(h, wn_ref[...], preferred_element_type=F32)
    spread(nq_ref, p, 4)
    spread(nqr_ref, jnp.concatenate([rope(pair(p, 0)), rope(pair(p, 1))], axis=1), 4)
    one = (lane == HEAD_DIM).astype(F32)
    slc = rope_k(pair(p, 2))
    nslc_ref[:, 0:HP] = (jnp.where(low, slc, 0.0) + rope_ref[:, 3 * HP:4 * HP]).astype(BF16)
    nslc_ref[:, HP:2 * HP] = (head(slc, 1) + one).astype(BF16)
    win = rope_k(pair(p, 3))
    nwin_ref[:, 0:HP] = head(win, 0).astype(BF16)
    nwin_ref[:, HP:2 * HP] = (head(win, 1) + one).astype(BF16)
    ncmp_ref[...] = pair(p, 4)

    p = jnp.dot(h, wr_ref[...], preferred_element_type=F32)
    spread(rq_ref, jnp.concatenate([rope(pair(p, 0)), rope(pair(p, 1))], axis=1), 4)
    spread(rk_ref, jnp.concatenate([rope(pair(p, 2)), rope(pair(p, 3))], axis=1), 4)
    rv_ref[...] = p[:, 4 * HP:4 * HP + 512].astype(BF16)
    rg_ref[...] = p[:, 4 * HP + 512:4 * HP + 1024]

    p = jnp.dot(h, ws_ref[...], preferred_element_type=F32)
    spread(sq_ref, jnp.concatenate([rope(pair(p, 0)), rope(pair(p, 1))], axis=1), 4)
    spread(skv_ref, rope(pair(p, 2)), 2)
    for hd in range(2):
        skv_ref[:, (2 + hd) * HP:(3 + hd) * HP] = (head(pair(p, 3), hd) + one).astype(BF16)

    p = jnp.dot(h, ww_ref[...], preferred_element_type=F32)
    @pl.when(i == 0)
    def _():
        carry_ref[...] = jnp.zeros(carry_ref.shape, F32)

    first = jnp.where(i % tiles_per_seq == 0, 0.0, carry_ref[0:1, :])
    row = lax.broadcasted_iota(jnp.int32, p.shape, 0)
    prev = jnp.where(row == 0, first, pltpu.roll(p, 1, 0))
    carry_ref[0:1, :] = p[tm - 1:tm, :]
    xm = p + mu_ref[...] * (prev - p)
    WC = RWKV_HEADS * HEAD_DIM
    r = xm[:, 0:WC]
    k = xm[:, WC:2 * WC]
    v = xm[:, 2 * WC:3 * WC]
    lora = xm[:, 3 * WC:3 * WC + HP]
    gl = xm[:, 3 * WC + HP:3 * WC + 2 * HP]
    z = -(w0_ref[...] + _dot(jnp.tanh(lora), w2_ref[...]))
    softplus = jnp.maximum(z, 0.0) + jnp.log(1.0 + jnp.exp(-jnp.abs(z)))
    logw = -jnp.exp(-softplus - 0.5)
    a = jax.nn.sigmoid(a0_ref[...] + _dot(lora, a2_ref[...]))
    g = _dot(jax.nn.sigmoid(gl), g2_ref[...])
    kkr = k * kk_ref[...]
    k2 = k * (1.0 + (a - 1.0) * ka_ref[...])
    W4 = 4 * HP
    spread(rw_ref, r, 4)
    spread(rw_ref, k2, 4, col0=W4)
    spread(rw_ref, v, 4, col0=2 * W4)
    for hd in range(4):
        kj = head(kkr, hd)
        kkj = kj / jnp.maximum(jnp.sqrt(jnp.sum(kj * kj, axis=-1, keepdims=True)), 1e-12)
        rw_ref[:, 3 * W4 + hd * HP:3 * W4 + (hd + 1) * HP] = -kkj
        rw_ref[:, 4 * W4 + hd * HP:4 * W4 + (hd + 1) * HP] = kkj * head(a, hd)
    spread(rw_ref, logw, 4, col0=5 * W4)
    spread(rw_ref, g, 4, col0=6 * W4)


def _inproj(x, S, pre_g, rope_tab, wn, wr, ww, ws, mu, w0, w2, a0, a2, g2, k_k, k_a, tm=256):
    T, D = x.shape
    row = lambda w: pl.BlockSpec((tm, w), lambda i: (i, 0))
    tps = S // tm
    outs = [4 * HP, 4 * HP, 2 * HP, 2 * HP, HP, 4 * HP, 4 * HP, 512, 512, RW_OUT, 4 * HP, 4 * HP]
    dts = [BF16, BF16, BF16, BF16, F32, F32, F32, BF16, F32, F32, BF16, BF16]
    return pl.pallas_call(
        functools.partial(_inproj_kernel, tiles_per_seq=tps),
        grid=(T // tm,),
        in_specs=[row(D), _const_spec((1, D)),
                  pl.BlockSpec((tm, 4 * HP), lambda i: (i % tps, 0)),
                  _const_spec(wn.shape), _const_spec(wr.shape), _const_spec(ww.shape), _const_spec(ws.shape),
                  _const_spec(mu.shape), _const_spec(w0.shape), _const_spec(w2.shape), _const_spec(a0.shape),
                  _const_spec(a2.shape), _const_spec(g2.shape), _const_spec(k_k.shape), _const_spec(k_a.shape)],
        out_specs=[row(w) for w in outs],
        out_shape=[jax.ShapeDtypeStruct((T, w), dt) for w, dt in zip(outs, dts)],
        scratch_shapes=[pltpu.VMEM((8, RW_COLS), F32)],
        compiler_params=_cp("arbitrary"),
    )(x, pre_g, rope_tab, wn, wr, ww, ws, mu, w0, w2, a0, a2, g2, k_k, k_a)


def _compress_kernel(h_ref, add_ref, w1_ref, w2_ref, kc_ref, vc_ref):
    hb = h_ref[...]
    n_half = hb.shape[0]
    for t, o_ref in ((0, kc_ref), (1, vc_ref)):
        top = _dot(hb + add_ref[2 * t:2 * t + 1, :], w1_ref[2 * t])
        bot = _dot(hb + add_ref[2 * t + 1:2 * t + 2, :], w1_ref[2 * t + 1])
        pre = top + pltpu.roll(bot, n_half - 1, 0)
        o_ref[...] = _dot(jax.nn.gelu(pre), w2_ref[t])


def _compress(hb, add, w1, w2):
    B, n_half, W = hb.shape
    return pl.pallas_call(
        _compress_kernel,
        grid=(B,),
        in_specs=[pl.BlockSpec((None, n_half, W), lambda b: (b, 0, 0)),
                  _const_spec(add.shape), _const_spec(w1.shape), _const_spec(w2.shape)],
        out_specs=[pl.BlockSpec((None, n_half, HP), lambda b: (b, 0, 0))] * 2,
        out_shape=[jax.ShapeDtypeStruct((B, n_half, HP), F32)] * 2,
        compiler_params=_cp("arbitrary"),
    )(hb, add, w1, w2)


def _cmp_kernel(q_ref, kc_ref, vc_ref, ovl_ref, o_ref, pen_ref, *, n_top):
    i = pl.program_id(1)
    tq = q_ref.shape[0]
    n_half = kc_ref.shape[0]
    n_sel = ovl_ref.shape[0]
    kc = kc_ref[...]
    vc = vc_ref[...]
    tpos = i * tq + lax.broadcasted_iota(jnp.int32, (tq, n_half), 0)
    n_id = lax.broadcasted_iota(jnp.int32, (tq, n_half), 1)
    mask = (n_id * NSA_CMP_STRIDE + NSA_CMP_LEN - 1 <= tpos) & (n_id < n_half - 1)
    maskf = mask.astype(F32)
    heads = range(NSA_HEADS)
    s = [jnp.where(mask, _dot_nt(q_ref[:, hd * HP:(hd + 1) * HP], kc), NEG_INF) for hd in heads]
    m = [jnp.max(s[hd], axis=-1, keepdims=True) for hd in heads]
    p = [jnp.exp2(s[hd] - m[hd]) * maskf for hd in heads]
    p = [p[hd] / jnp.maximum(jnp.sum(p[hd], axis=-1, keepdims=True), 1e-30) for hd in heads]
    o = [_dot(p[hd], vc) for hd in heads]
    imp = [_dot_nt(ovl_ref[...], p[hd]) for hd in heads]
    imp_t = (imp[0] + imp[1]) + (imp[2] + imp[3])
    for hd in heads:
        o_ref[:, hd * HP:(hd + 1) * HP] = o[hd]
    blk = lax.broadcasted_iota(jnp.int32, (n_sel, tq), 0)
    cur = (i * tq + lax.broadcasted_iota(jnp.int32, (n_sel, tq), 1)) // NSA_SEL_BLOCK
    forced = (blk == 0) | (blk == cur) | (blk == cur - 1)
    valid = blk <= cur
    score = jnp.where(forced, FORCED_SCORE, jnp.where(valid, imp_t, -FORCED_SCORE))
    SUB = 8
    grp = [score[g * SUB:(g + 1) * SUB, :] for g in range(n_sel // SUB)]
    rank_g = [jnp.zeros((SUB, tq), F32) for _ in grp]
    sub_row = lax.broadcasted_iota(jnp.int32, (SUB, tq), 0)
    for b in range(n_sel):
        sb = score[b:b + 1, :]
        for g, sc in enumerate(grp):
            if g * SUB > b:
                before = sb >= sc
            elif (g + 1) * SUB - 1 < b:
                before = sb > sc
            else:
                before = (sb > sc) | ((sb == sc) & (sub_row > b - g * SUB))
            rank_g[g] = rank_g[g] + jnp.where(before, 1.0, 0.0)
    rank = jnp.concatenate(rank_g, axis=0)
    pen_t = jnp.where((rank < n_top) & valid, 0.0, NEG_INF)
    pieces = [jnp.zeros((HEAD_DIM, tq), F32), pen_t]
    if HP - HEAD_DIM - n_sel:
        pieces.append(jnp.zeros((HP - HEAD_DIM - n_sel, tq), F32))
    pen_ref[...] = jnp.concatenate(pieces, axis=0).T.astype(BF16)


def _cmp_attn(q, kc, vc, ovl_t, n_top, tq=2 * ATT_BLOCK):
    B, S, W = q.shape
    n_half = kc.shape[1]
    n_sel = ovl_t.shape[0]
    return pl.pallas_call(
        functools.partial(_cmp_kernel, n_top=n_top),
        grid=(B, S // tq),
        in_specs=[pl.BlockSpec((None, tq, W), lambda b, i: (b, i, 0)),
                  pl.BlockSpec((None, n_half, HP), lambda b, i: (b, 0, 0)),
                  pl.BlockSpec((None, n_half, HP), lambda b, i: (b, 0, 0)),
                  _const_spec(ovl_t.shape)],
        out_specs=[pl.BlockSpec((None, tq, W), lambda b, i: (b, i, 0)),
                   pl.BlockSpec((None, tq, HP), lambda b, i: (b, i, 0))],
        out_shape=[jax.ShapeDtypeStruct((B, S, W), F32), jax.ShapeDtypeStruct((B, S, HP), BF16)],
        compiler_params=_cp("arbitrary", "arbitrary"),
    )(q, kc, vc, ovl_t)


def _band_kernel(*refs, n_kv, group, window, back, use_sink):
    if use_sink:
        q_ref, kv_ref, sink_ref, o_ref = refs
    else:
        q_ref, kv_ref, o_ref = refs
    i = pl.program_id(1)
    tq = q_ref.shape[0]
    span = back + tq
    start = pl.multiple_of(jnp.maximum(i * tq - back, 0), ATT_BLOCK)
    tpos = i * tq + lax.broadcasted_iota(jnp.int32, (tq, span), 0)
    diff = tpos - (start + lax.broadcasted_iota(jnp.int32, (tq, span), 1))
    mask = (diff >= 0) & (diff < window)
    heads = range(n_kv * group)
    hsl = lambda hd: slice(hd * HP, (hd + 1) * HP)
    k = [kv_ref[pl.ds(start, span), hsl(kh)] for kh in range(n_kv)]
    v = [kv_ref[pl.ds(start, span), hsl(n_kv + kh)] for kh in range(n_kv)]
    s = [jnp.where(mask, _dot_nt(q_ref[:, hsl(hd)], k[hd // group]), NEG_INF) for hd in heads]
    m = [jnp.max(s[hd], axis=-1, keepdims=True) for hd in heads]
    if use_sink:
        m = [jnp.maximum(m[hd], sink_ref[hd]) for hd in heads]
    p = [jnp.exp2((s[hd] - m[hd]).astype(BF16)) for hd in heads]
    o = [jnp.dot(p[hd], v[hd // group], preferred_element_type=F32) for hd in heads]
    out_lane = lax.broadcasted_iota(jnp.int32, (tq, HP), 1)
    for hd in heads:
        l = jnp.sum(jnp.where(out_lane == HEAD_DIM, o[hd], 0.0), axis=-1, keepdims=True)
        if use_sink:
            l = l + jnp.exp2(sink_ref[hd] - m[hd])
        o_ref[:, hsl(hd)] = jnp.where(out_lane < HEAD_DIM, o[hd] / l, 0.0).astype(o_ref.dtype)


def _band(q, kv, n_kv, group, window, out_dtype, sinks=None, tq=2 * ATT_BLOCK):
    B, S, W = q.shape
    back = -(-(window - 1) // ATT_BLOCK) * ATT_BLOCK
    assert S >= back + tq
    use_sink = sinks is not None
    args = [q, kv]
    in_specs = [pl.BlockSpec((None, tq, W), lambda b, i: (b, i, 0)),
                pl.BlockSpec((None, S, kv.shape[2]), lambda b, i: (b, 0, 0))]
    if use_sink:
        args.append(sinks)
        in_specs.append(pl.BlockSpec(memory_space=pltpu.SMEM))
    return pl.pallas_call(
        functools.partial(_band_kernel, n_kv=n_kv, group=group, window=window, back=back, use_sink=use_sink),
        grid=(B, S // tq),
        in_specs=in_specs,
        out_specs=pl.BlockSpec((None, tq, W), lambda b, i: (b, i, 0)),
        out_shape=jax.ShapeDtypeStruct((B, S, W), out_dtype),
        compiler_params=_cp("arbitrary", "arbitrary"),
    )(*args)


def _slc_kernel(q_ref, kv_ref, pen_ref, o_ref, *, tk):
    i = pl.program_id(1)
    tq = q_ref.shape[0]
    n_t = (i * tq + tq - 1) // tk + 1
    heads = range(NSA_HEADS)
    pen = pen_ref[...].astype(F32)
    q = [(q_ref[:, hd * HP:(hd + 1) * HP].astype(F32) + pen).astype(BF16) for hd in heads]

    def step(j, carry, causal):
        r0 = pl.multiple_of(j * tk, tk)
        k_t = kv_ref[pl.ds(r0, tk), 0:HP]
        v_t = kv_ref[pl.ds(r0, tk), HP:2 * HP]
        s = [_dot_nt(q[hd], k_t) for hd in heads]
        if causal:
            tpos = i * tq + lax.broadcasted_iota(jnp.int32, (tq, tk), 0)
            ok = tpos >= j * tk + lax.broadcasted_iota(jnp.int32, (tq, tk), 1)
            s = [jnp.where(ok, s[hd], NEG_INF) for hd in heads]
        m_new = [jnp.maximum(carry[hd][0], jnp.max(s[hd], axis=-1, keepdims=True)) for hd in heads]
        p = [jnp.exp2((s[hd] - m_new[hd]).astype(BF16)) for hd in heads]
        pv = [jnp.dot(p[hd], v_t, preferred_element_type=F32) for hd in heads]
        return tuple((m_new[hd], carry[hd][1] * jnp.exp2(carry[hd][0] - m_new[hd]) + pv[hd]) for hd in heads)

    init = tuple((jnp.full((tq, 1), NEG_INF, F32), jnp.zeros((tq, HP), F32)) for _ in heads)
    mid = lax.fori_loop(0, n_t - 1, lambda j, c: step(j, c, False), init)
    fin = step(n_t - 1, mid, True)
    out_lane = lax.broadcasted_iota(jnp.int32, (tq, HP), 1)
    for hd in heads:
        acc = fin[hd][1]
        l = jnp.sum(jnp.where(out_lane == HEAD_DIM, acc, 0.0), axis=-1, keepdims=True)
        o_ref[:, hd * HP:(hd + 1) * HP] = jnp.where(out_lane < HEAD_DIM, acc / l, 0.0)


def _slc(q, kv, pen, tq=ATT_BLOCK, tk=SLC_TILE):
    B, S, W = q.shape
    assert tk % tq == 0 and S % tk == 0
    return pl.pallas_call(
        functools.partial(_slc_kernel, tk=tk),
        grid=(B, S // tq),
        in_specs=[pl.BlockSpec((None, tq, W), lambda b, i: (b, i, 0)),
                  pl.BlockSpec((None, S, kv.shape[2]), lambda b, i: (b, 0, 0)),
                  pl.BlockSpec((None, tq, HP), lambda b, i: (b, i, 0))],
        out_specs=pl.BlockSpec((None, tq, W), lambda b, i: (b, i, 0)),
        out_shape=jax.ShapeDtypeStruct((B, S, W), F32),
        compiler_params=_cp("arbitrary", "arbitrary"),
    )(q, kv, pen)


def _ret_kernel(q_ref, k_ref, v_ref, g_ref, dm_ref, zeta_ref, xi_ref, gch_ref, gn_ref, o_ref, st_ref, *, n_chunks):
    c = pl.program_id(1)

    @pl.when(c == 0)
    def _():
        st_ref[...] = jnp.zeros(st_ref.shape, F32)

    C = RET_CHUNK
    pairs = [(ch, hd) for ch in range(n_chunks) for hd in range(RET_HEADS)]
    rows = lambda ch: slice(ch * C, (ch + 1) * C)
    q = {p: q_ref[rows(p[0]), p[1] * HP:(p[1] + 1) * HP] for p in pairs}
    k = {p: k_ref[rows(p[0]), p[1] * HP:(p[1] + 1) * HP] for p in pairs}
    v = {p: v_ref[rows(p[0]), p[1] * RET_DV:(p[1] + 1) * RET_DV] for p in pairs}
    inner = {p: _dot_nt(q[p], k[p]) * dm_ref[p[1]] for p in pairs}
    o = {p: _dot(inner[p], v[p]) for p in pairs}
    kv = {p: _dot_tn(k[p] * zeta_ref[p[1]], v[p]) for p in pairs}
    for hd in range(RET_HEADS):
        state = st_ref[hd]
        for ch in range(n_chunks):
            p = (ch, hd)
            o[p] = o[p] + _dot(q[p] * xi_ref[hd], state)
            state = state * gch_ref[hd] + kv[p]
        st_ref[hd] = state
    for p in pairs:
        ch, hd = p
        mu = jnp.mean(o[p], axis=-1, keepdims=True)
        d = o[p] - mu
        var = jnp.mean(d * d, axis=-1, keepdims=True)
        on = d * lax.rsqrt(var + GN_EPS) * gn_ref[:, hd * RET_DV:(hd + 1) * RET_DV]
        gate = jax.nn.silu(g_ref[rows(ch), hd * RET_DV:(hd + 1) * RET_DV])
        o_ref[rows(ch), hd * RET_DV:(hd + 1) * RET_DV] = (gate * on).astype(BF16)


def _retention(q, k, v, g, dmask, zeta, xi, gch, gn_g, n_chunks=2):
    B, S, _ = q.shape
    C = n_chunks * RET_CHUNK
    blk = lambda w: pl.BlockSpec((None, C, w), lambda b, c: (b, c, 0))
    return pl.pallas_call(
        functools.partial(_ret_kernel, n_chunks=n_chunks),
        grid=(B, S // C),
        in_specs=[blk(4 * HP), blk(4 * HP), blk(512), blk(512),
                  _const_spec(dmask.shape), _const_spec(zeta.shape), _const_spec(xi.shape),
                  _const_spec(gch.shape), _const_spec(gn_g.shape)],
        out_specs=blk(512),
        out_shape=jax.ShapeDtypeStruct((B, S, 512), BF16),
        scratch_shapes=[pltpu.VMEM((RET_HEADS, HP, RET_DV), F32)],
        compiler_params=_cp("arbitrary", "arbitrary"),
    )(q, k, v, g, dmask, zeta, xi, gch, gn_g)


def _rwkv_kernel(rw_ref, rk_ref, lng_ref, lnb_ref, o_ref, z_ref, *, n_chunks):
    c = pl.program_id(1)

    @pl.when(c == 0)
    def _():
        z_ref[...] = jnp.zeros(z_ref.shape, F32)

    L = RWKV_CHUNK
    W4 = 4 * HP
    ri = lax.broadcasted_iota(jnp.int32, (2 * L, 2 * L), 0)
    ci = lax.broadcasted_iota(jnp.int32, (2 * L, 2 * L), 1) % L
    a_mask = ((ri < L) & (ri > ci)) | (ri - L >= ci)
    tri = (lax.broadcasted_iota(jnp.int32, (L, L), 0) >= lax.broadcasted_iota(jnp.int32, (L, L), 1)).astype(BF16)
    dr = lax.broadcasted_iota(jnp.int32, (HP, HP), 0)
    dc = lax.broadcasted_iota(jnp.int32, (HP, HP), 1)
    real = (lax.broadcasted_iota(jnp.int32, (L, HP), 1) < HEAD_DIM).astype(F32)
    zeros = jnp.zeros((L, HP), F32)
    tdot = lambda t: jnp.dot(tri, t, preferred_element_type=F32)
    pairs = [(ch, hd) for ch in range(n_chunks) for hd in range(RWKV_HEADS)]
    col = lambda ch, hd, gi: rw_ref[ch * L:(ch + 1) * L, gi * W4 + hd * HP:gi * W4 + (hd + 1) * HP]
    cum_all = []
    for ch in range(n_chunks):
        lw_all = rw_ref[ch * L:(ch + 1) * L, 5 * W4:6 * W4]
        hi = lw_all.astype(BF16)
        rem = lw_all - hi.astype(F32)
        mid = rem.astype(BF16)
        lo = (rem - mid.astype(F32)).astype(BF16)
        cum_all.append(tdot(hi) + tdot(mid) + tdot(lo))
    cum = {p: cum_all[p[0]][:, p[1] * HP:(p[1] + 1) * HP] for p in pairs}
    r, k, v, a, b, lw = ({p: col(p[0], p[1], gi) for p in pairs} for gi in range(6))
    e_inv = {p: jnp.exp(-cum[p]) for p in pairs}
    at = {p: a[p] * jnp.exp(cum[p] - lw[p]) for p in pairs}
    rt = {p: r[p] * jnp.exp(cum[p]) for p in pairs}
    amat = {p: jnp.where(a_mask, _dot_nt(jnp.concatenate([at[p], rt[p]], axis=0),
                                         jnp.concatenate([b[p] * e_inv[p], k[p] * e_inv[p]], axis=0)), 0.0)
            for p in pairs}
    half = lambda t: pltpu.roll(t, HEAD_DIM, 1)
    x = {p: at[p] + half(_dot(amat[p][0:L], jnp.concatenate([zeros, v[p]], axis=0))) for p in pairs}
    npow = {p: amat[p][0:L, 0:L] for p in pairs}
    for s in range(int(math.log2(L))):
        x = {p: x[p] + _dot(npow[p], x[p]) for p in pairs}
        if (2 << s) < L:
            npow = {p: _dot(npow[p], npow[p]) for p in pairs}
    rhs = {p: jnp.concatenate([x[p], half(v[p])], axis=0) for p in pairs}
    yq = {p: _dot(amat[p][L:2 * L], rhs[p]) for p in pairs}
    mn = {}
    for p in pairs:
        e_rem = jnp.exp(cum[p][L - 1:L, :] - cum[p])
        mn[p] = _dot_tn(jnp.concatenate([b[p] * e_rem, k[p] * e_rem], axis=0), rhs[p])
    low = (dc < HEAD_DIM).astype(F32)
    y = {}
    for hd in range(RWKV_HEADS):
        z = z_ref[hd]
        for ch in range(n_chunks):
            p = (ch, hd)
            y[p] = half(yq[p]) * real + _dot(rt[p] + yq[p] * real, z)
            decay = jnp.where(dr == dc, jnp.exp(cum[p][L - 1:L, :]), 0.0)
            z = _dot(decay + mn[p] * low, z) + half(mn[p]) * low
        z_ref[hd] = z
    for p in pairs:
        ch, hd = p
        mu = jnp.sum(y[p], axis=-1, keepdims=True) * (1.0 / HEAD_DIM)
        d = (y[p] - mu) * real
        var = jnp.sum(d * d, axis=-1, keepdims=True) * (1.0 / HEAD_DIM)
        yn = d * lax.rsqrt(var + RWKV_LN_EPS) * lng_ref[:, hd * HP:(hd + 1) * HP] + lnb_ref[:, hd * HP:(hd + 1) * HP]
        bonus = jnp.sum(r[p] * k[p] * rk_ref[:, hd * HP:(hd + 1) * HP], axis=-1, keepdims=True) * v[p]
        o_ref[ch * L:(ch + 1) * L, hd * HP:(hd + 1) * HP] = ((yn + bonus) * col(ch, hd, 6)).astype(o_ref.dtype)


def _rwkv(rw, r_k, ln_g, ln_b, n_chunks=8):
    B, S, _ = rw.shape
    L = n_chunks * RWKV_CHUNK
    return pl.pallas_call(
        functools.partial(_rwkv_kernel, n_chunks=n_chunks),
        grid=(B, S // L),
        in_specs=[pl.BlockSpec((None, L, RW_OUT), lambda b, c: (b, c, 0)),
                  _const_spec(r_k.shape), _const_spec(ln_g.shape), _const_spec(ln_b.shape)],
        out_specs=pl.BlockSpec((None, L, 4 * HP), lambda b, c: (b, c, 0)),
        out_shape=jax.ShapeDtypeStruct((B, S, 4 * HP), BF16),
        scratch_shapes=[pltpu.VMEM((RWKV_HEADS, HP, HP), F32)],
        compiler_params=_cp("arbitrary", "arbitrary"),
    )(rw, r_k, ln_g, ln_b)


def _merge_kernel(x_ref, ocmp_ref, oslc_ref, owin_ref, yret_ref, yrw_ref, yswa_ref,
                  pre_ref, post_ref, wg_ref, wn_ref, wr_ref, ww_ref, ws_ref, wo_ref, o_ref):
    x = x_ref[...]
    D = x.shape[1]
    h = _rms(x, pre_ref[...]).astype(BF16)
    tm = x.shape[0]
    gate = lambda j, w: jax.nn.sigmoid(jnp.dot(h, wg_ref[:, j:j + w], preferred_element_type=F32))
    g_nsa = gate(4 * D, HP)
    bc = lambda j: jnp.broadcast_to(g_nsa[:, j:j + 1], (tm, HP))
    hsl = lambda hd: slice(hd * HP, (hd + 1) * HP)
    y_nsa = [bc(3 * hd) * ocmp_ref[:, hsl(hd)] + bc(3 * hd + 1) * oslc_ref[:, hsl(hd)]
             + bc(3 * hd + 2) * owin_ref[:, hsl(hd)] for hd in range(NSA_HEADS)]
    low = lax.broadcasted_iota(jnp.int32, (tm, HP), 1) < HEAD_DIM
    pack = lambda hs: jnp.concatenate([jnp.where(low, hs[j], pltpu.roll(hs[j + 1], HEAD_DIM, 1))
                                       for j in range(0, len(hs), 2)], axis=1)
    heads_of = lambda ref: [ref[:, hsl(hd)].astype(F32) for hd in range(ref.shape[1] // HP)]
    merged = gate(0, D) * _dot(pack(y_nsa), wn_ref[...])
    merged = merged + gate(D, D) * _dot(yret_ref[...], wr_ref[...])
    merged = merged + gate(2 * D, D) * _dot(pack(heads_of(yrw_ref)), ww_ref[...])
    merged = merged + gate(3 * D, D) * _dot(pack(heads_of(yswa_ref)), ws_ref[...])
    o_ref[...] = x + _rms(_dot(merged, wo_ref[...]), post_ref[...])


def _merge(x, ocmp, oslc, owin, yret, yrw, yswa, pre_g, post_g, wg, wn, wr, ww, ws, wo, tm=256):
    T, D = x.shape
    row = lambda w: pl.BlockSpec((tm, w), lambda i: (i, 0))
    return pl.pallas_call(
        _merge_kernel,
        grid=(T // tm,),
        in_specs=[row(D), row(4 * HP), row(4 * HP), row(4 * HP), row(512), row(4 * HP), row(4 * HP),
                  _const_spec((1, D)), _const_spec((1, D)), _const_spec(wg.shape), _const_spec(wn.shape),
                  _const_spec(wr.shape), _const_spec(ww.shape), _const_spec(ws.shape), _const_spec(wo.shape)],
        out_specs=row(D),
        out_shape=jax.ShapeDtypeStruct((T, D), F32),
        compiler_params=_cp("arbitrary"),
    )(x, ocmp, oslc, owin, yret, yrw, yswa, pre_g, post_g, wg, wn, wr, ww, ws, wo)


def _column_layout():
    spec = (
        ('nsa_q', 256), ('nsa_k_cmp', 64), ('nsa_v_cmp', 64), ('nsa_k_slc', 64), ('nsa_v_slc', 64),
        ('nsa_k_win', 64), ('nsa_v_win', 64), ('nsa_gate', 12),
        ('ret_q', 256), ('ret_k', 256), ('ret_v', 512), ('ret_g', 512),
        ('rwkv', 1024),
        ('swa_q', 256), ('swa_k', 128), ('swa_v', 128),
        ('branch_gate', 4 * D_MODEL),
    )
    layout, start = {}, 0
    for name, width in spec:
        layout[name] = (start, start + width)
        start += width
    return layout


def _pad_heads(w, axis=-1):
    w = jnp.moveaxis(w, axis, -1)
    lead = w.shape[:-1]
    n = w.shape[-1] // HEAD_DIM
    w = w.reshape(lead + (n, HEAD_DIM))
    w = jnp.pad(w, [(0, 0)] * len(lead) + [(0, 0), (0, HP - HEAD_DIM)])
    return jnp.moveaxis(w.reshape(lead + (n * HP,)), -1, axis)


def _pad_cols(w, width):
    return jnp.pad(w, [(0, 0)] * (w.ndim - 1) + [(0, width - w.shape[-1])])


def _rope_table(S):
    half = HEAD_DIM // 2
    inv_freq = jnp.power(ROPE_THETA, -jnp.arange(half, dtype=F32) * 2.0 / HEAD_DIM)
    ang = jnp.arange(S, dtype=jnp.int32).astype(F32)[:, None] * inv_freq[None, :]
    cos, sin, z = jnp.cos(ang), jnp.sin(ang), jnp.zeros((S, half), F32)
    assert S // NSA_SEL_BLOCK <= HP - HEAD_DIM
    blk = jnp.asarray(np.arange(S)[:, None] // NSA_SEL_BLOCK == np.arange(HP)[None, :] - HEAD_DIM, F32)
    return jnp.concatenate([cos, cos, cos, cos, sin, z, sin, z, z, sin, z, sin, blk], axis=1)


def _retention_tables():
    H, C = RET_HEADS, RET_CHUNK
    log_g = jnp.log(1.0 - jnp.power(2.0, -5.0 - jnp.arange(H, dtype=F32)))
    i = jnp.arange(C, dtype=F32)
    diff = i[:, None] - i[None, :]
    dmask = jnp.where(diff >= 0, jnp.exp(log_g[:, None, None] * jnp.maximum(diff, 0.0)), 0.0)
    zeta = jnp.exp(log_g[:, None] * (C - 1 - i)[None, :])
    xi = jnp.exp(log_g[:, None] * (i + 1.0)[None, :])
    gch = jnp.exp(log_g * C)
    bc = lambda t: jnp.broadcast_to(t[:, :, None], (H, C, HP))
    return dmask, bc(zeta), bc(xi), jnp.broadcast_to(gch[:, None, None], (H, HP, RET_DV))


def kernel(x, ffn1_pre_g, ffn1_post_g, ffn1_w_gate, ffn1_w_up, ffn1_w_down, mix_pre_g, mix_post_g, w_in, nsa_cmp_pos_k, nsa_cmp_pos_v, nsa_cmp_k_w1, nsa_cmp_k_w2, nsa_cmp_v_w1, nsa_cmp_v_w2, ret_gn_g, rwkv_mu, rwkv_w0, rwkv_w2, rwkv_a0, rwkv_a2, rwkv_g2, rwkv_k_k, rwkv_k_a, rwkv_r_k, rwkv_ln_g, rwkv_ln_b, swa_sinks, w_br_nsa, w_br_ret, w_br_rwkv, w_br_swa, w_out, ffn2_pre_g, ffn2_post_g, ffn2_w_gate, ffn2_w_up, ffn2_w_down):
    B, S, D = x.shape
    T = B * S
    depth = w_in.shape[0]
    lay = _column_layout()
    n_half = S // NSA_CMP_STRIDE
    n_sel = S // NSA_SEL_BLOCK
    n_top = min(NSA_TOP_N, n_sel)
    scale = HEAD_DIM ** -0.5
    sm_scale = scale * LOG2E
    rope_tab = _rope_table(S)
    dmask, zeta, xi, gch = _retention_tables()
    cs = np.arange(n_half) * NSA_CMP_STRIDE
    ss = np.arange(n_sel) * NSA_SEL_BLOCK
    ovl = ((cs[None, :] <= ss[:, None] + NSA_SEL_BLOCK - 1) & (cs[None, :] + NSA_CMP_LEN - 1 >= ss[:, None])
           & (np.arange(n_half)[None, :] < n_half - 1))
    ovl_t = jnp.asarray(ovl, BF16)
    row = lambda t: t.reshape(1, -1)

    xf = x.reshape(T, D)
    for l in range(depth):
        wl = w_in[l]
        c = lambda name: wl[:, lay[name][0]:lay[name][1]]
        xf = _ffn(xf, row(ffn1_pre_g[l]), row(ffn1_post_g[l]), ffn1_w_gate[l].astype(BF16),
                  ffn1_w_up[l].astype(BF16), ffn1_w_down[l].astype(BF16))
        w_nsa = jnp.concatenate([c('nsa_q') * sm_scale, c('nsa_k_slc'), c('nsa_v_slc'), c('nsa_k_win'),
                                 c('nsa_v_win'), c('nsa_k_cmp'), c('nsa_v_cmp')], axis=1).astype(BF16)
        w_ret = jnp.concatenate([c('ret_q'), c('ret_k') * scale, c('ret_v'), c('ret_g')], axis=1).astype(BF16)
        w_rw = c('rwkv').astype(BF16)
        w_swa = jnp.concatenate([c('swa_q') * sm_scale, c('swa_k'), c('swa_v')], axis=1).astype(BF16)
        w2_p = jnp.pad(rwkv_w2[l], ((0, HP - rwkv_w2.shape[1]), (0, 0)))
        a2_p = jnp.pad(rwkv_a2[l], ((HP - rwkv_a2.shape[1], 0), (0, 0)))
        (nsa_q, nsa_qr, kv_slc, kv_win, kv_cmp, ret_q, ret_k, ret_v, ret_g, rw, swa_q, swa_kv) = _inproj(
            xf, S, row(mix_pre_g[l]), rope_tab, w_nsa, w_ret, w_rw, w_swa, row(rwkv_mu[l]),
            row(rwkv_w0[l]), w2_p, row(rwkv_a0[l]), a2_p, rwkv_g2[l], row(rwkv_k_k[l]), row(rwkv_k_a[l]))
        b3 = lambda t: t.reshape(B, S, t.shape[-1])

        half = NSA_CMP_STRIDE * HEAD_DIM
        z16 = jnp.zeros((NSA_CMP_STRIDE, HEAD_DIM), F32)
        emb = lambda pk, pv: jnp.concatenate([pk, pv], axis=1).reshape(1, -1)
        add = jnp.concatenate([emb(nsa_cmp_pos_k[l][:16], z16), emb(nsa_cmp_pos_k[l][16:], z16),
                               emb(z16, nsa_cmp_pos_v[l][:16]), emb(z16, nsa_cmp_pos_v[l][16:])], axis=0)

        def w1_embed(w1_half, is_v):
            w = w1_half.reshape(NSA_CMP_STRIDE, HEAD_DIM, NSA_CMP_HIDDEN)
            zz = jnp.zeros_like(w)
            parts = (zz, w) if is_v else (w, zz)
            return jnp.concatenate(parts, axis=1).reshape(2 * half, NSA_CMP_HIDDEN)

        w1 = jnp.stack([w1_embed(nsa_cmp_k_w1[l][:half], False), w1_embed(nsa_cmp_k_w1[l][half:], False),
                        w1_embed(nsa_cmp_v_w1[l][:half], True), w1_embed(nsa_cmp_v_w1[l][half:], True)]).astype(BF16)
        w2 = jnp.stack([_pad_cols(nsa_cmp_k_w2[l], HP), _pad_cols(nsa_cmp_v_w2[l], HP)]).astype(BF16)
        kc, vc = _compress(kv_cmp.reshape(B, n_half, NSA_CMP_STRIDE * 2 * HEAD_DIM), add, w1, w2)
        o_cmp, pen = _cmp_attn(b3(nsa_q), kc, vc, ovl_t, n_top)
        o_slc = _slc(b3(nsa_qr), b3(kv_slc), pen)
        o_win = _band(b3(nsa_qr), b3(kv_win), 1, NSA_HEADS, NSA_WINDOW, F32)
        y_ret = _retention(b3(ret_q), b3(ret_k), b3(ret_v), b3(ret_g), dmask, zeta, xi, gch, row(ret_gn_g[l]))
        y_rw = _rwkv(b3(rw), row(_pad_heads(rwkv_r_k[l].reshape(-1))), row(_pad_heads(rwkv_ln_g[l])),
                     row(_pad_heads(rwkv_ln_b[l])))
        y_swa = _band(b3(swa_q), b3(swa_kv), SWA_KV_HEADS, SWA_HEADS // SWA_KV_HEADS, SWA_WINDOW, BF16,
                      sinks=swa_sinks[l] * LOG2E)
        w_gate = jnp.concatenate([c('branch_gate'), _pad_cols(c('nsa_gate'), HP)], axis=1).astype(BF16)
        f2 = lambda t: t.reshape(T, t.shape[-1])
        xf = _merge(xf, f2(o_cmp), f2(o_slc), f2(o_win), f2(y_ret), f2(y_rw), f2(y_swa),
                    row(mix_pre_g[l]), row(mix_post_g[l]), w_gate,
                    w_br_nsa[l].astype(BF16), w_br_ret[l].astype(BF16),
                    w_br_rwkv[l].astype(BF16), w_br_swa[l].astype(BF16),
                    w_out[l].astype(BF16))
        xf = _ffn(xf, row(ffn2_pre_g[l]), row(ffn2_post_g[l]), ffn2_w_gate[l].astype(BF16),
                  ffn2_w_up[l].astype(BF16), ffn2_w_down[l].astype(BF16))
    return xf.reshape(B, S, D)
```

```python
import functools
import math

import numpy as np
import jax
import jax.numpy as jnp
from jax import lax
from jax.experimental import pallas as pl
from jax.experimental.pallas import tpu as pltpu

F32 = jnp.float32
BF16 = jnp.bfloat16
HIGHEST = lax.Precision.HIGHEST

D_MODEL = 1024
HEAD_DIM = 64
HP = 128
ROPE_THETA = 10000.0
RMS_EPS = 1e-6
GN_EPS = 1e-5
RWKV_LN_EPS = 64e-5
D_FF = 2816
NEG_INF = -1e30
FORCED_SCORE = 1e9
LOG2E = math.log2(math.e)

NSA_HEADS = 4
NSA_CMP_LEN = 32
NSA_CMP_STRIDE = 16
NSA_CMP_HIDDEN = 256
NSA_SEL_BLOCK = 64
NSA_TOP_N = 16
NSA_WINDOW = 512
RET_HEADS = 4
RET_DV = 128
RET_CHUNK = 128
RWKV_HEADS = 4
RWKV_CHUNK = 64
SWA_HEADS = 4
SWA_KV_HEADS = 2
SWA_WINDOW = 128
ATT_BLOCK = 128
SLC_TILE = 1024

VMEM_LIMIT = 56 * 1024 * 1024


def _cp(*sem):
    return pltpu.CompilerParams(dimension_semantics=sem, vmem_limit_bytes=VMEM_LIMIT)


def _const_spec(shape):
    nd = len(shape)
    return pl.BlockSpec(shape, lambda *_: (0,) * nd, pipeline_mode=pl.Buffered(1))


def _dot(a, b):
    return jnp.dot(a.astype(BF16), b.astype(BF16), preferred_element_type=F32)


def _dot_nt(a, b):
    return lax.dot_general(a.astype(BF16), b.astype(BF16), (((1,), (1,)), ((), ())),
                           preferred_element_type=F32)


def _dot_tn(a, b):
    return lax.dot_general(a.astype(BF16), b.astype(BF16), (((0,), (0,)), ((), ())),
                           preferred_element_type=F32)


def _dot_hi(a, b):
    return jnp.dot(a, b, precision=HIGHEST, preferred_element_type=F32)


def _rms(x, g):
    return x * lax.rsqrt(jnp.mean(x * x, axis=-1, keepdims=True) + RMS_EPS) * g


def _rope(t, cos, sin_lo, sin_hi):
    return t * cos + pltpu.roll(t, 32, 1) * sin_hi - pltpu.roll(t, HP - 32, 1) * sin_lo


def _ffn_kernel(x_ref, pre_ref, post_ref, wg_ref, wu_ref, wd_ref, o_ref, *, ck):
    x = x_ref[...]
    h = _rms(x, pre_ref[...]).astype(BF16)
    acc = jnp.zeros(x.shape, F32)
    for c in range(D_FF // ck):
        g = jnp.dot(h, wg_ref[:, c * ck:(c + 1) * ck], preferred_element_type=F32)
        u = jnp.dot(h, wu_ref[:, c * ck:(c + 1) * ck], preferred_element_type=F32)
        a = (jax.nn.silu(g) * u).astype(BF16)
        acc = acc + jnp.dot(a, wd_ref[c * ck:(c + 1) * ck, :], preferred_element_type=F32)
    o_ref[...] = x + 0.5 * _rms(acc, post_ref[...])


def _ffn(x, pre_g, post_g, wg, wu, wd, tm=512):
    T, D = x.shape
    return pl.pallas_call(
        functools.partial(_ffn_kernel, ck=256),
        grid=(T // tm,),
        in_specs=[pl.BlockSpec((tm, D), lambda i: (i, 0)),
                  _const_spec((1, D)), _const_spec((1, D)),
                  _const_spec((D, D_FF)), _const_spec((D, D_FF)), _const_spec((D_FF, D))],
        out_specs=pl.BlockSpec((tm, D), lambda i: (i, 0)),
        out_shape=jax.ShapeDtypeStruct((T, D), F32),
        compiler_params=_cp("arbitrary"),
    )(x, pre_g, post_g, wg, wu, wd)


NSA_COLS = 640
RET_COLS = 1536
RW_COLS = 1024
SWA_COLS = 512
RW_OUT = 7 * 4 * HP


def _inproj_kernel(x_ref, pre_ref, rope_ref, w_ref,
                   mu_ref, w0_ref, w2_ref, a0_ref, a2_ref, g2_ref, kk_ref, ka_ref,
                   nq_ref, nqr_ref, nslc_ref, nwin_ref, ncmp_ref,
                   rq_ref, rk_ref, rv_ref, rg_ref, rw_ref, sq_ref, skv_ref,
                   carry_ref, *, tiles_per_seq):
    i = pl.program_id(0)
    x = x_ref[...]
    tm = x.shape[0]
    h = _rms(x, pre_ref[...]).astype(BF16)
    lane = lax.broadcasted_iota(jnp.int32, (tm, HP), 1)
    low = lane < HEAD_DIM
    cos = rope_ref[:, 0:HP]
    s_lo = rope_ref[:, HP:2 * HP]
    s_hi = rope_ref[:, 2 * HP:3 * HP]
    rope = lambda t: _rope(t, cos, s_lo, s_hi)
    rope_k = lambda t: _rope(t, jnp.where(low, cos, 1.0), jnp.where(low, s_lo, 0.0), jnp.where(low, s_hi, 0.0))
    pair = lambda t, j: t[:, j * HP:(j + 1) * HP]
    head = lambda t, hd: jnp.where(low, pair(t, hd // 2) if hd % 2 == 0 else pltpu.roll(pair(t, hd // 2), HEAD_DIM, 1), 0.0)

    def spread(ref, t, n_heads, col0=0):
        for hd in range(n_heads):
            ref[:, col0 + hd * HP:col0 + (hd + 1) * HP] = head(t, hd).astype(ref.dtype)

    sm_scale = HEAD_DIM ** -0.5 * LOG2E
    proj = lambda c0, c1: jnp.dot(h, w_ref[:, c0:c1], preferred_element_type=F32)

    p = proj(0, NSA_COLS)
    qn = p[:, 0:2 * HP] * sm_scale
    spread(nq_ref, qn, 4)
    spread(nqr_ref, jnp.concatenate([rope(pair(qn, 0)), rope(pair(qn, 1))], axis=1), 4)
    ncmp_ref[...] = pair(p, 2)
    one = (lane == HEAD_DIM).astype(F32)
    slc = rope_k(pair(p, 3))
    nslc_ref[:, 0:HP] = (jnp.where(low, slc, 0.0) + rope_ref[:, 3 * HP:4 * HP]).astype(BF16)
    nslc_ref[:, HP:2 * HP] = (head(slc, 1) + one).astype(BF16)
    win = rope_k(pair(p, 4))
    nwin_ref[:, 0:HP] = head(win, 0).astype(BF16)
    nwin_ref[:, HP:2 * HP] = (head(win, 1) + one).astype(BF16)

    p = proj(NSA_COLS, NSA_COLS + RET_COLS)
    spread(rq_ref, jnp.concatenate([rope(pair(p, 0)), rope(pair(p, 1))], axis=1), 4)
    spread(rk_ref, jnp.concatenate([rope(pair(p, 2)), rope(pair(p, 3))], axis=1) * HEAD_DIM ** -0.5, 4)
    rv_ref[...] = p[:, 4 * HP:4 * HP + 512].astype(BF16)
    rg_ref[...] = p[:, 4 * HP + 512:4 * HP + 1024]

    p = proj(NSA_COLS + RET_COLS + RW_COLS, NSA_COLS + RET_COLS + RW_COLS + SWA_COLS)
    spread(sq_ref, jnp.concatenate([rope(pair(p, 0)), rope(pair(p, 1))], axis=1) * sm_scale, 4)
    spread(skv_ref, rope(pair(p, 2)), 2)
    for hd in range(2):
        skv_ref[:, (2 + hd) * HP:(3 + hd) * HP] = (head(pair(p, 3), hd) + one).astype(BF16)

    p = proj(NSA_COLS + RET_COLS, NSA_COLS + RET_COLS + RW_COLS)
    @pl.when(i == 0)
    def _():
        carry_ref[...] = jnp.zeros(carry_ref.shape, F32)

    first = jnp.where(i % tiles_per_seq == 0, 0.0, carry_ref[0:1, :])
    row = lax.broadcasted_iota(jnp.int32, p.shape, 0)
    prev = jnp.where(row == 0, first, pltpu.roll(p, 1, 0))
    carry_ref[0:1, :] = p[tm - 1:tm, :]
    xm = p + mu_ref[...] * (prev - p)
    WC = RWKV_HEADS * HEAD_DIM
    r = xm[:, 0:WC]
    k = xm[:, WC:2 * WC]
    v = xm[:, 2 * WC:3 * WC]
    lora = xm[:, 3 * WC:3 * WC + HP]
    gl = xm[:, 3 * WC + HP:3 * WC + 2 * HP]
    z = -(w0_ref[...] + _dot(jnp.tanh(lora), w2_ref[...]))
    softplus = jnp.maximum(z, 0.0) + jnp.log(1.0 + jnp.exp(-jnp.abs(z)))
    logw = -jnp.exp(-softplus - 0.5)
    a = jax.nn.sigmoid(a0_ref[...] + _dot(lora, a2_ref[...]))
    g = _dot(jax.nn.sigmoid(gl), g2_ref[...])
    kkr = k * kk_ref[...]
    k2 = k * (1.0 + (a - 1.0) * ka_ref[...])
    W4 = 4 * HP
    spread(rw_ref, r, 4)
    spread(rw_ref, k2, 4, col0=W4)
    spread(rw_ref, v, 4, col0=2 * W4)
    for hd in range(4):
        kj = head(kkr, hd)
        kkj = kj * (1.0 / jnp.maximum(jnp.sqrt(jnp.sum(kj * kj, axis=-1, keepdims=True)), 1e-12))
        rw_ref[:, 3 * W4 + hd * HP:3 * W4 + (hd + 1) * HP] = -kkj
        rw_ref[:, 4 * W4 + hd * HP:4 * W4 + (hd + 1) * HP] = kkj * head(a, hd)
    spread(rw_ref, logw, 4, col0=5 * W4)
    spread(rw_ref, g, 4, col0=6 * W4)


def _inproj(x, S, pre_g, rope_tab, w, mu, w0, w2, a0, a2, g2, k_k, k_a, tm=256):
    T, D = x.shape
    row = lambda w: pl.BlockSpec((tm, w), lambda i: (i, 0))
    tps = S // tm
    outs = [4 * HP, 4 * HP, 2 * HP, 2 * HP, HP, 4 * HP, 4 * HP, 512, 512, RW_OUT, 4 * HP, 4 * HP]
    dts = [BF16, BF16, BF16, BF16, F32, F32, F32, BF16, F32, F32, BF16, BF16]
    return pl.pallas_call(
        functools.partial(_inproj_kernel, tiles_per_seq=tps),
        grid=(T // tm,),
        in_specs=[row(D), _const_spec((1, D)),
                  pl.BlockSpec((tm, 4 * HP), lambda i: (i % tps, 0)),
                  _const_spec(w.shape),
                  _const_spec(mu.shape), _const_spec(w0.shape), _const_spec(w2.shape), _const_spec(a0.shape),
                  _const_spec(a2.shape), _const_spec(g2.shape), _const_spec(k_k.shape), _const_spec(k_a.shape)],
        out_specs=[row(w) for w in outs],
        out_shape=[jax.ShapeDtypeStruct((T, w), dt) for w, dt in zip(outs, dts)],
        scratch_shapes=[pltpu.VMEM((8, RW_COLS), F32)],
        compiler_params=_cp("arbitrary"),
    )(x, pre_g, rope_tab, w, mu, w0, w2, a0, a2, g2, k_k, k_a)


def _compress_kernel(h_ref, add_ref, w1_ref, w2_ref, kc_ref, vc_ref):
    hb = h_ref[...]
    n_half = hb.shape[0]
    for t, o_ref in ((0, kc_ref), (1, vc_ref)):
        top = _dot(hb + add_ref[2 * t:2 * t + 1, :], w1_ref[2 * t])
        bot = _dot(hb + add_ref[2 * t + 1:2 * t + 2, :], w1_ref[2 * t + 1])
        pre = top + pltpu.roll(bot, n_half - 1, 0)
        o_ref[...] = _dot(jax.nn.gelu(pre), w2_ref[t])


def _compress(hb, add, w1, w2):
    B, n_half, W = hb.shape
    return pl.pallas_call(
        _compress_kernel,
        grid=(B,),
        in_specs=[pl.BlockSpec((None, n_half, W), lambda b: (b, 0, 0)),
                  _const_spec(add.shape), _const_spec(w1.shape), _const_spec(w2.shape)],
        out_specs=[pl.BlockSpec((None, n_half, HP), lambda b: (b, 0, 0))] * 2,
        out_shape=[jax.ShapeDtypeStruct((B, n_half, HP), F32)] * 2,
        compiler_params=_cp("arbitrary"),
    )(hb, add, w1, w2)


def _cmp_kernel(q_ref, kc_ref, vc_ref, ovl_ref, o_ref, pen_ref, *, n_top):
    i = pl.program_id(1)
    tq = q_ref.shape[0]
    n_half = kc_ref.shape[0]
    n_sel = ovl_ref.shape[0]
    kc = kc_ref[...]
    vc = vc_ref[...]
    tpos = i * tq + lax.broadcasted_iota(jnp.int32, (tq, n_half), 0)
    n_id = lax.broadcasted_iota(jnp.int32, (tq, n_half), 1)
    mask = (n_id * NSA_CMP_STRIDE + NSA_CMP_LEN - 1 <= tpos) & (n_id < n_half - 1)
    maskf = mask.astype(F32)
    heads = range(NSA_HEADS)
    s = [jnp.where(mask, _dot_nt(q_ref[:, hd * HP:(hd + 1) * HP], kc), NEG_INF) for hd in heads]
    m = [jnp.max(s[hd], axis=-1, keepdims=True) for hd in heads]
    p = [jnp.exp2(s[hd] - m[hd]) * maskf for hd in heads]
    p = [p[hd] * (1.0 / jnp.maximum(jnp.sum(p[hd], axis=-1, keepdims=True), 1e-30)) for hd in heads]
    o = [_dot(p[hd], vc) for hd in heads]
    imp = [_dot_nt(ovl_ref[...], p[hd]) for hd in heads]
    imp_t = (imp[0] + imp[1]) + (imp[2] + imp[3])
    for hd in heads:
        o_ref[:, hd * HP:(hd + 1) * HP] = o[hd]
    blk = lax.broadcasted_iota(jnp.int32, (n_sel, tq), 0)
    cur = (i * tq + lax.broadcasted_iota(jnp.int32, (n_sel, tq), 1)) // NSA_SEL_BLOCK
    forced = (blk == 0) | (blk == cur) | (blk == cur - 1)
    valid = blk <= cur
    score = jnp.where(forced, FORCED_SCORE, jnp.where(valid, imp_t, -FORCED_SCORE))
    SUB = 8
    grp = [score[g * SUB:(g + 1) * SUB, :] for g in range(n_sel // SUB)]
    rank_g = [jnp.zeros((SUB, tq), F32) for _ in grp]
    sub_row = lax.broadcasted_iota(jnp.int32, (SUB, tq), 0)
    for b in range(n_sel):
        sb = score[b:b + 1, :]
        for g, sc in enumerate(grp):
            if g * SUB > b:
                before = sb >= sc
            elif (g + 1) * SUB - 1 < b:
                before = sb > sc
            else:
                before = (sb > sc) | ((sb == sc) & (sub_row > b - g * SUB))
            rank_g[g] = rank_g[g] + jnp.where(before, 1.0, 0.0)
    rank = jnp.concatenate(rank_g, axis=0)
    pen_t = jnp.where((rank < n_top) & valid, 0.0, NEG_INF)
    pieces = [jnp.zeros((HEAD_DIM, tq), F32), pen_t]
    if HP - HEAD_DIM - n_sel:
        pieces.append(jnp.zeros((HP - HEAD_DIM - n_sel, tq), F32))
    pen_ref[...] = jnp.concatenate(pieces, axis=0).T.astype(BF16)


def _cmp_attn(q, kc, vc, ovl_t, n_top, tq=2 * ATT_BLOCK):
    B, S, W = q.shape
    n_half = kc.shape[1]
    n_sel = ovl_t.shape[0]
    return pl.pallas_call(
        functools.partial(_cmp_kernel, n_top=n_top),
        grid=(B, S // tq),
        in_specs=[pl.BlockSpec((None, tq, W), lambda b, i: (b, i, 0)),
                  pl.BlockSpec((None, n_half, HP), lambda b, i: (b, 0, 0)),
                  pl.BlockSpec((None, n_half, HP), lambda b, i: (b, 0, 0)),
                  _const_spec(ovl_t.shape)],
        out_specs=[pl.BlockSpec((None, tq, W), lambda b, i: (b, i, 0)),
                   pl.BlockSpec((None, tq, HP), lambda b, i: (b, i, 0))],
        out_shape=[jax.ShapeDtypeStruct((B, S, W), F32), jax.ShapeDtypeStruct((B, S, HP), BF16)],
        compiler_params=_cp("arbitrary", "arbitrary"),
    )(q, kc, vc, ovl_t)


def _band_kernel(*refs, n_kv, group, window, back, use_sink):
    if use_sink:
        q_ref, kv_ref, sink_ref, o_ref = refs
    else:
        q_ref, kv_ref, o_ref = refs
    i = pl.program_id(1)
    tq = q_ref.shape[0]
    span = back + tq
    start = pl.multiple_of(jnp.maximum(i * tq - back, 0), ATT_BLOCK)
    tpos = i * tq + lax.broadcasted_iota(jnp.int32, (tq, span), 0)
    diff = tpos - (start + lax.broadcasted_iota(jnp.int32, (tq, span), 1))
    mask = (diff >= 0) & (diff < window)
    heads = range(n_kv * group)
    hsl = lambda hd: slice(hd * HP, (hd + 1) * HP)
    k = [kv_ref[pl.ds(start, span), hsl(kh)] for kh in range(n_kv)]
    v = [kv_ref[pl.ds(start, span), hsl(n_kv + kh)] for kh in range(n_kv)]
    s = [jnp.where(mask, _dot_nt(q_ref[:, hsl(hd)], k[hd // group]), NEG_INF) for hd in heads]
    m = [jnp.max(s[hd], axis=-1, keepdims=True) for hd in heads]
    if use_sink:
        m = [jnp.maximum(m[hd], sink_ref[hd]) for hd in heads]
    p = [jnp.exp2((s[hd] - m[hd]).astype(BF16)) for hd in heads]
    o = [jnp.dot(p[hd], v[hd // group], preferred_element_type=F32) for hd in heads]
    out_lane = lax.broadcasted_iota(jnp.int32, (tq, HP), 1)
    for hd in heads:
        l = jnp.sum(jnp.where(out_lane == HEAD_DIM, o[hd], 0.0), axis=-1, keepdims=True)
        if use_sink:
            l = l + jnp.exp2(sink_ref[hd] - m[hd])
        o_ref[:, hsl(hd)] = jnp.where(out_lane < HEAD_DIM, o[hd] * (1.0 / l), 0.0).astype(o_ref.dtype)


def _band(q, kv, n_kv, group, window, out_dtype, sinks=None, tq=2 * ATT_BLOCK):
    B, S, W = q.shape
    back = -(-(window - 1) // ATT_BLOCK) * ATT_BLOCK
    assert S >= back + tq
    use_sink = sinks is not None
    args = [q, kv]
    in_specs = [pl.BlockSpec((None, tq, W), lambda b, i: (b, i, 0)),
                pl.BlockSpec((None, S, kv.shape[2]), lambda b, i: (b, 0, 0))]
    if use_sink:
        args.append(sinks)
        in_specs.append(pl.BlockSpec(memory_space=pltpu.SMEM))
    return pl.pallas_call(
        functools.partial(_band_kernel, n_kv=n_kv, group=group, window=window, back=back, use_sink=use_sink),
        grid=(B, S // tq),
        in_specs=in_specs,
        out_specs=pl.BlockSpec((None, tq, W), lambda b, i: (b, i, 0)),
        out_shape=jax.ShapeDtypeStruct((B, S, W), out_dtype),
        compiler_params=_cp("arbitrary", "arbitrary"),
    )(*args)


def _slc_kernel(q_ref, kv_ref, pen_ref, o_ref, *, tk):
    i = pl.program_id(1)
    tq = q_ref.shape[0]
    n_t = (i * tq + tq - 1) // tk + 1
    heads = range(NSA_HEADS)
    pen = pen_ref[...].astype(F32)
    q = [(q_ref[:, hd * HP:(hd + 1) * HP].astype(F32) + pen).astype(BF16) for hd in heads]

    def step(j, carry, causal):
        r0 = pl.multiple_of(j * tk, tk)
        k_t = kv_ref[pl.ds(r0, tk), 0:HP]
        v_t = kv_ref[pl.ds(r0, tk), HP:2 * HP]
        s = [_dot_nt(q[hd], k_t) for hd in heads]
        if causal:
            tpos = i * tq + lax.broadcasted_iota(jnp.int32, (tq, tk), 0)
            ok = tpos >= j * tk + lax.broadcasted_iota(jnp.int32, (tq, tk), 1)
            s = [jnp.where(ok, s[hd], NEG_INF) for hd in heads]
        m_new = [jnp.maximum(carry[hd][0], jnp.max(s[hd], axis=-1, keepdims=True)) for hd in heads]
        p = [jnp.exp2((s[hd] - m_new[hd]).astype(BF16)) for hd in heads]
        pv = [jnp.dot(p[hd], v_t, preferred_element_type=F32) for hd in heads]
        return tuple((m_new[hd], carry[hd][1] * jnp.exp2(carry[hd][0] - m_new[hd]) + pv[hd]) for hd in heads)

    init = tuple((jnp.full((tq, 1), NEG_INF, F32), jnp.zeros((tq, HP), F32)) for _ in heads)
    mid = lax.fori_loop(0, n_t - 1, lambda j, c: step(j, c, False), init)
    fin = step(n_t - 1, mid, True)
    out_lane = lax.broadcasted_iota(jnp.int32, (tq, HP), 1)
    for hd in heads:
        acc = fin[hd][1]
        l = jnp.sum(jnp.where(out_lane == HEAD_DIM, acc, 0.0), axis=-1, keepdims=True)
        o_ref[:, hd * HP:(hd + 1) * HP] = jnp.where(out_lane < HEAD_DIM, acc * (1.0 / l), 0.0)


def _slc(q, kv, pen, tq=2 * ATT_BLOCK, tk=SLC_TILE):
    B, S, W = q.shape
    assert tk % tq == 0 and S % tk == 0
    return pl.pallas_call(
        functools.partial(_slc_kernel, tk=tk),
        grid=(B, S // tq),
        in_specs=[pl.BlockSpec((None, tq, W), lambda b, i: (b, i, 0)),
                  pl.BlockSpec((None, S, kv.shape[2]), lambda b, i: (b, 0, 0)),
                  pl.BlockSpec((None, tq, HP), lambda b, i: (b, i, 0))],
        out_specs=pl.BlockSpec((None, tq, W), lambda b, i: (b, i, 0)),
        out_shape=jax.ShapeDtypeStruct((B, S, W), F32),
        compiler_params=_cp("arbitrary", "arbitrary"),
    )(q, kv, pen)


def _ret_kernel(q_ref, k_ref, v_ref, g_ref, dm_ref, zeta_ref, xi_ref, gch_ref, gn_ref, o_ref, st_ref, *, n_chunks):
    c = pl.program_id(1)

    @pl.when(c == 0)
    def _():
        st_ref[...] = jnp.zeros(st_ref.shape, F32)

    C = RET_CHUNK
    pairs = [(ch, hd) for ch in range(n_chunks) for hd in range(RET_HEADS)]
    rows = lambda ch: slice(ch * C, (ch + 1) * C)
    q = {p: q_ref[rows(p[0]), p[1] * HP:(p[1] + 1) * HP] for p in pairs}
    k = {p: k_ref[rows(p[0]), p[1] * HP:(p[1] + 1) * HP] for p in pairs}
    v = {p: v_ref[rows(p[0]), p[1] * RET_DV:(p[1] + 1) * RET_DV] for p in pairs}
    inner = {p: _dot_nt(q[p], k[p]) * dm_ref[p[1]] for p in pairs}
    o = {p: _dot(inner[p], v[p]) for p in pairs}
    kv = {p: _dot_tn(k[p] * zeta_ref[p[1]], v[p]) for p in pairs}
    for hd in range(RET_HEADS):
        state = st_ref[hd]
        for ch in range(n_chunks):
            p = (ch, hd)
            o[p] = o[p] + _dot(q[p] * xi_ref[hd], state)
            state = state * gch_ref[hd] + kv[p]
        st_ref[hd] = state
    for p in pairs:
        ch, hd = p
        mu = jnp.mean(o[p], axis=-1, keepdims=True)
        d = o[p] - mu
        var = jnp.mean(d * d, axis=-1, keepdims=True)
        on = d * lax.rsqrt(var + GN_EPS) * gn_ref[:, hd * RET_DV:(hd + 1) * RET_DV]
        gate = jax.nn.silu(g_ref[rows(ch), hd * RET_DV:(hd + 1) * RET_DV])
        o_ref[rows(ch), hd * RET_DV:(hd + 1) * RET_DV] = (gate * on).astype(BF16)


def _retention(q, k, v, g, dmask, zeta, xi, gch, gn_g, n_chunks=4):
    B, S, _ = q.shape
    C = n_chunks * RET_CHUNK
    blk = lambda w: pl.BlockSpec((None, C, w), lambda b, c: (b, c, 0))
    return pl.pallas_call(
        functools.partial(_ret_kernel, n_chunks=n_chunks),
        grid=(B, S // C),
        in_specs=[blk(4 * HP), blk(4 * HP), blk(512), blk(512),
                  _const_spec(dmask.shape), _const_spec(zeta.shape), _const_spec(xi.shape),
                  _const_spec(gch.shape), _const_spec(gn_g.shape)],
        out_specs=blk(512),
        out_shape=jax.ShapeDtypeStruct((B, S, 512), BF16),
        scratch_shapes=[pltpu.VMEM((RET_HEADS, HP, RET_DV), F32)],
        compiler_params=_cp("arbitrary", "arbitrary"),
    )(q, k, v, g, dmask, zeta, xi, gch, gn_g)


def _rwkv_kernel(rw_ref, rk_ref, lng_ref, lnb_ref, o_ref, z_ref, *, n_chunks):
    c = pl.program_id(1)

    @pl.when(c == 0)
    def _():
        z_ref[...] = jnp.zeros(z_ref.shape, F32)

    L = RWKV_CHUNK
    W4 = 4 * HP
    ri = lax.broadcasted_iota(jnp.int32, (2 * L, 2 * L), 0)
    ci = lax.broadcasted_iota(jnp.int32, (2 * L, 2 * L), 1) % L
    a_mask = ((ri < L) & (ri > ci)) | (ri - L >= ci)
    tri = (lax.broadcasted_iota(jnp.int32, (L, L), 0) >= lax.broadcasted_iota(jnp.int32, (L, L), 1)).astype(BF16)
    dr = lax.broadcasted_iota(jnp.int32, (HP, HP), 0)
    dc = lax.broadcasted_iota(jnp.int32, (HP, HP), 1)
    real = (lax.broadcasted_iota(jnp.int32, (L, HP), 1) < HEAD_DIM).astype(F32)
    zeros = jnp.zeros((L, HP), F32)
    tdot = lambda t: jnp.dot(tri, t, preferred_element_type=F32)
    pairs = [(ch, hd) for ch in range(n_chunks) for hd in range(RWKV_HEADS)]
    col = lambda ch, hd, gi: rw_ref[ch * L:(ch + 1) * L, gi * W4 + hd * HP:gi * W4 + (hd + 1) * HP]
    cum_all = []
    for ch in range(n_chunks):
        lw_all = rw_ref[ch * L:(ch + 1) * L, 5 * W4:6 * W4]
        hi = lw_all.astype(BF16)
        rem = lw_all - hi.astype(F32)
        mid = rem.astype(BF16)
        lo = (rem - mid.astype(F32)).astype(BF16)
        cum_all.append(tdot(hi) + tdot(mid) + tdot(lo))
    cum = {p: cum_all[p[0]][:, p[1] * HP:(p[1] + 1) * HP] for p in pairs}
    r, k, v, a, b, lw = ({p: col(p[0], p[1], gi) for p in pairs} for gi in range(6))
    e_inv = {p: jnp.exp(-cum[p]) for p in pairs}
    at = {p: a[p] * jnp.exp(cum[p] - lw[p]) for p in pairs}
    rt = {p: r[p] * jnp.exp(cum[p]) for p in pairs}
    amat = {p: jnp.where(a_mask, _dot_nt(jnp.concatenate([at[p], rt[p]], axis=0),
                                         jnp.concatenate([b[p] * e_inv[p], k[p] * e_inv[p]], axis=0)), 0.0)
            for p in pairs}
    half = lambda t: pltpu.roll(t, HEAD_DIM, 1)
    x = {p: at[p] + half(_dot(amat[p][0:L], jnp.concatenate([zeros, v[p]], axis=0))) for p in pairs}
    npow = {p: amat[p][0:L, 0:L] for p in pairs}
    for s in range(int(math.log2(L))):
        x = {p: x[p] + _dot(npow[p], x[p]) for p in pairs}
        if (2 << s) < L:
            npow = {p: _dot(npow[p], npow[p]) for p in pairs}
    rhs = {p: jnp.concatenate([x[p], half(v[p])], axis=0) for p in pairs}
    yq = {p: _dot(amat[p][L:2 * L], rhs[p]) for p in pairs}
    mn = {}
    for p in pairs:
        e_rem = jnp.exp(cum[p][L - 1:L, :] - cum[p])
        mn[p] = _dot_tn(jnp.concatenate([b[p] * e_rem, k[p] * e_rem], axis=0), rhs[p])
    low = (dc < HEAD_DIM).astype(F32)
    y = {}
    for hd in range(RWKV_HEADS):
        z = z_ref[hd]
        for ch in range(n_chunks):
            p = (ch, hd)
            y[p] = half(yq[p]) * real + _dot(rt[p] + yq[p] * real, z)
            decay = jnp.where(dr == dc, jnp.exp(cum[p][L - 1:L, :]), 0.0)
            z = _dot(decay + mn[p] * low, z) + half(mn[p]) * low
        z_ref[hd] = z
    for p in pairs:
        ch, hd = p
        mu = jnp.sum(y[p], axis=-1, keepdims=True) * (1.0 / HEAD_DIM)
        d = (y[p] - mu) * real
        var = jnp.sum(d * d, axis=-1, keepdims=True) * (1.0 / HEAD_DIM)
        yn = d * lax.rsqrt(var + RWKV_LN_EPS) * lng_ref[:, hd * HP:(hd + 1) * HP] + lnb_ref[:, hd * HP:(hd + 1) * HP]
        bonus = jnp.sum(r[p] * k[p] * rk_ref[:, hd * HP:(hd + 1) * HP], axis=-1, keepdims=True) * v[p]
        o_ref[ch * L:(ch + 1) * L, hd * HP:(hd + 1) * HP] = ((yn + bonus) * col(ch, hd, 6)).astype(o_ref.dtype)


def _rwkv(rw, r_k, ln_g, ln_b, n_chunks=8):
    B, S, _ = rw.shape
    L = n_chunks * RWKV_CHUNK
    return pl.pallas_call(
        functools.partial(_rwkv_kernel, n_chunks=n_chunks),
        grid=(B, S // L),
        in_specs=[pl.BlockSpec((None, L, RW_OUT), lambda b, c: (b, c, 0)),
                  _const_spec(r_k.shape), _const_spec(ln_g.shape), _const_spec(ln_b.shape)],
        out_specs=pl.BlockSpec((None, L, 4 * HP), lambda b, c: (b, c, 0)),
        out_shape=jax.ShapeDtypeStruct((B, S, 4 * HP), BF16),
        scratch_shapes=[pltpu.VMEM((RWKV_HEADS, HP, HP), F32)],
        compiler_params=_cp("arbitrary", "arbitrary"),
    )(rw, r_k, ln_g, ln_b)


def _merge_kernel(x_ref, ocmp_ref, oslc_ref, owin_ref, yret_ref, yrw_ref, yswa_ref,
                  pre_ref, post_ref, wg_ref, wn_ref, wr_ref, ww_ref, ws_ref, wo_ref, o_ref):
    x = x_ref[...]
    D = x.shape[1]
    h = _rms(x, pre_ref[...]).astype(BF16)
    tm = x.shape[0]
    gate = lambda j, w: jax.nn.sigmoid(jnp.dot(h, wg_ref[:, j:j + w], preferred_element_type=F32))
    g_nsa = gate(4 * D, HP)
    bc = lambda j: jnp.broadcast_to(g_nsa[:, j:j + 1], (tm, HP))
    hsl = lambda hd: slice(hd * HP, (hd + 1) * HP)
    y_nsa = [bc(3 * hd) * ocmp_ref[:, hsl(hd)] + bc(3 * hd + 1) * oslc_ref[:, hsl(hd)]
             + bc(3 * hd + 2) * owin_ref[:, hsl(hd)] for hd in range(NSA_HEADS)]
    low = lax.broadcasted_iota(jnp.int32, (tm, HP), 1) < HEAD_DIM
    pack = lambda hs: jnp.concatenate([jnp.where(low, hs[j], pltpu.roll(hs[j + 1], HEAD_DIM, 1))
                                       for j in range(0, len(hs), 2)], axis=1)
    heads_of = lambda ref: [ref[:, hsl(hd)].astype(F32) for hd in range(ref.shape[1] // HP)]
    merged = gate(0, D) * _dot(pack(y_nsa), wn_ref[...])
    merged = merged + gate(D, D) * _dot(yret_ref[...], wr_ref[...])
    merged = merged + gate(2 * D, D) * _dot(pack(heads_of(yrw_ref)), ww_ref[...])
    merged = merged + gate(3 * D, D) * _dot(pack(heads_of(yswa_ref)), ws_ref[...])
    o_ref[...] = x + _rms(_dot(merged, wo_ref[...]), post_ref[...])


def _merge(x, ocmp, oslc, owin, yret, yrw, yswa, pre_g, post_g, wg, wn, wr, ww, ws, wo, tm=256):
    T, D = x.shape
    row = lambda w: pl.BlockSpec((tm, w), lambda i: (i, 0))
    return pl.pallas_call(
        _merge_kernel,
        grid=(T // tm,),
        in_specs=[row(D), row(4 * HP), row(4 * HP), row(4 * HP), row(512), row(4 * HP), row(4 * HP),
                  _const_spec((1, D)), _const_spec((1, D)), _const_spec(wg.shape), _const_spec(wn.shape),
                  _const_spec(wr.shape), _const_spec(ww.shape), _const_spec(ws.shape), _const_spec(wo.shape)],
        out_specs=row(D),
        out_shape=jax.ShapeDtypeStruct((T, D), F32),
        compiler_params=_cp("arbitrary"),
    )(x, ocmp, oslc, owin, yret, yrw, yswa, pre_g, post_g, wg, wn, wr, ww, ws, wo)


def _column_layout():
    spec = (
        ('nsa_q', 256), ('nsa_k_cmp', 64), ('nsa_v_cmp', 64), ('nsa_k_slc', 64), ('nsa_v_slc', 64),
        ('nsa_k_win', 64), ('nsa_v_win', 64), ('nsa_gate', 12),
        ('ret_q', 256), ('ret_k', 256), ('ret_v', 512), ('ret_g', 512),
        ('rwkv', 1024),
        ('swa_q', 256), ('swa_k', 128), ('swa_v', 128),
        ('branch_gate', 4 * D_MODEL),
    )
    layout, start = {}, 0
    for name, width in spec:
        layout[name] = (start, start + width)
        start += width
    return layout


def _pad_heads(w, axis=-1):
    w = jnp.moveaxis(w, axis, -1)
    lead = w.shape[:-1]
    n = w.shape[-1] // HEAD_DIM
    w = w.reshape(lead + (n, HEAD_DIM))
    w = jnp.pad(w, [(0, 0)] * len(lead) + [(0, 0), (0, HP - HEAD_DIM)])
    return jnp.moveaxis(w.reshape(lead + (n * HP,)), -1, axis)


def _pad_cols(w, width):
    return jnp.pad(w, [(0, 0)] * (w.ndim - 1) + [(0, width - w.shape[-1])])


def _rope_table(S):
    half = HEAD_DIM // 2
    inv_freq = jnp.power(ROPE_THETA, -jnp.arange(half, dtype=F32) * 2.0 / HEAD_DIM)
    ang = jnp.arange(S, dtype=jnp.int32).astype(F32)[:, None] * inv_freq[None, :]
    cos, sin, z = jnp.cos(ang), jnp.sin(ang), jnp.zeros((S, half), F32)
    assert S // NSA_SEL_BLOCK <= HP - HEAD_DIM
    blk = jnp.asarray(np.arange(S)[:, None] // NSA_SEL_BLOCK == np.arange(HP)[None, :] - HEAD_DIM, F32)
    return jnp.concatenate([cos, cos, cos, cos, sin, z, sin, z, z, sin, z, sin, blk], axis=1)


def _retention_tables():
    H, C = RET_HEADS, RET_CHUNK
    log_g = jnp.log(1.0 - jnp.power(2.0, -5.0 - jnp.arange(H, dtype=F32)))
    i = jnp.arange(C, dtype=F32)
    diff = i[:, None] - i[None, :]
    dmask = jnp.where(diff >= 0, jnp.exp(log_g[:, None, None] * jnp.maximum(diff, 0.0)), 0.0)
    zeta = jnp.exp(log_g[:, None] * (C - 1 - i)[None, :])
    xi = jnp.exp(log_g[:, None] * (i + 1.0)[None, :])
    gch = jnp.exp(log_g * C)
    bc = lambda t: jnp.broadcast_to(t[:, :, None], (H, C, HP))
    return dmask, bc(zeta), bc(xi), jnp.broadcast_to(gch[:, None, None], (H, HP, RET_DV))


def kernel(x, ffn1_pre_g, ffn1_post_g, ffn1_w_gate, ffn1_w_up, ffn1_w_down, mix_pre_g, mix_post_g, w_in, nsa_cmp_pos_k, nsa_cmp_pos_v, nsa_cmp_k_w1, nsa_cmp_k_w2, nsa_cmp_v_w1, nsa_cmp_v_w2, ret_gn_g, rwkv_mu, rwkv_w0, rwkv_w2, rwkv_a0, rwkv_a2, rwkv_g2, rwkv_k_k, rwkv_k_a, rwkv_r_k, rwkv_ln_g, rwkv_ln_b, swa_sinks, w_br_nsa, w_br_ret, w_br_rwkv, w_br_swa, w_out, ffn2_pre_g, ffn2_post_g, ffn2_w_gate, ffn2_w_up, ffn2_w_down):
    B, S, D = x.shape
    T = B * S
    depth = w_in.shape[0]
    lay = _column_layout()
    n_half = S // NSA_CMP_STRIDE
    n_sel = S // NSA_SEL_BLOCK
    n_top = min(NSA_TOP_N, n_sel)
    rope_tab = _rope_table(S)
    dmask, zeta, xi, gch = _retention_tables()
    cs = np.arange(n_half) * NSA_CMP_STRIDE
    ss = np.arange(n_sel) * NSA_SEL_BLOCK
    ovl = ((cs[None, :] <= ss[:, None] + NSA_SEL_BLOCK - 1) & (cs[None, :] + NSA_CMP_LEN - 1 >= ss[:, None])
           & (np.arange(n_half)[None, :] < n_half - 1))
    ovl_t = jnp.asarray(ovl, BF16)
    row = lambda t: t.reshape(1, -1)

    w_bf = w_in.astype(BF16)
    xf = x.reshape(T, D)
    for l in range(depth):
        c = lambda name: w_bf[l, :, lay[name][0]:lay[name][1]]
        xf = _ffn(xf, row(ffn1_pre_g[l]), row(ffn1_post_g[l]), ffn1_w_gate[l].astype(BF16),
                  ffn1_w_up[l].astype(BF16), ffn1_w_down[l].astype(BF16))
        w_mix = jnp.concatenate([w_bf[l, :, lay['nsa_q'][0]:lay['nsa_gate'][0]],
                                 w_bf[l, :, lay['ret_q'][0]:lay['branch_gate'][0]]], axis=1)
        w2_p = jnp.pad(rwkv_w2[l], ((0, HP - rwkv_w2.shape[1]), (0, 0)))
        a2_p = jnp.pad(rwkv_a2[l], ((HP - rwkv_a2.shape[1], 0), (0, 0)))
        (nsa_q, nsa_qr, kv_slc, kv_win, kv_cmp, ret_q, ret_k, ret_v, ret_g, rw, swa_q, swa_kv) = _inproj(
            xf, S, row(mix_pre_g[l]), rope_tab, w_mix, row(rwkv_mu[l]),
            row(rwkv_w0[l]), w2_p, row(rwkv_a0[l]), a2_p, rwkv_g2[l], row(rwkv_k_k[l]), row(rwkv_k_a[l]))
        b3 = lambda t: t.reshape(B, S, t.shape[-1])

        half = NSA_CMP_STRIDE * HEAD_DIM
        z16 = jnp.zeros((NSA_CMP_STRIDE, HEAD_DIM), F32)
        emb = lambda pk, pv: jnp.concatenate([pk, pv], axis=1).reshape(1, -1)
        add = jnp.concatenate([emb(nsa_cmp_pos_k[l][:16], z16), emb(nsa_cmp_pos_k[l][16:], z16),
                               emb(z16, nsa_cmp_pos_v[l][:16]), emb(z16, nsa_cmp_pos_v[l][16:])], axis=0)

        def w1_embed(w1_half, is_v):
            w = w1_half.reshape(NSA_CMP_STRIDE, HEAD_DIM, NSA_CMP_HIDDEN)
            zz = jnp.zeros_like(w)
            parts = (zz, w) if is_v else (w, zz)
            return jnp.concatenate(parts, axis=1).reshape(2 * half, NSA_CMP_HIDDEN)

        w1 = jnp.stack([w1_embed(nsa_cmp_k_w1[l][:half], False), w1_embed(nsa_cmp_k_w1[l][half:], False),
                        w1_embed(nsa_cmp_v_w1[l][:half], True), w1_embed(nsa_cmp_v_w1[l][half:], True)]).astype(BF16)
        w2 = jnp.stack([_pad_cols(nsa_cmp_k_w2[l], HP), _pad_cols(nsa_cmp_v_w2[l], HP)]).astype(BF16)
        kc, vc = _compress(kv_cmp.reshape(B, n_half, NSA_CMP_STRIDE * 2 * HEAD_DIM), add, w1, w2)
        o_cmp, pen = _cmp_attn(b3(nsa_q), kc, vc, ovl_t, n_top)
        o_slc = _slc(b3(nsa_qr), b3(kv_slc), pen)
        o_win = _band(b3(nsa_qr), b3(kv_win), 1, NSA_HEADS, NSA_WINDOW, F32)
        y_ret = _retention(b3(ret_q), b3(ret_k), b3(ret_v), b3(ret_g), dmask, zeta, xi, gch, row(ret_gn_g[l]))
        y_rw = _rwkv(b3(rw), row(_pad_heads(rwkv_r_k[l].reshape(-1))), row(_pad_heads(rwkv_ln_g[l])),
                     row(_pad_heads(rwkv_ln_b[l])))
        y_swa = _band(b3(swa_q), b3(swa_kv), SWA_KV_HEADS, SWA_HEADS // SWA_KV_HEADS, SWA_WINDOW, BF16,
                      sinks=swa_sinks[l] * LOG2E)
        w_gate = jnp.concatenate([c('branch_gate'), _pad_cols(c('nsa_gate'), HP)], axis=1)
        f2 = lambda t: t.reshape(T, t.shape[-1])
        xf = _merge(xf, f2(o_cmp), f2(o_slc), f2(o_win), f2(y_ret), f2(y_rw), f2(y_swa),
                    row(mix_pre_g[l]), row(mix_post_g[l]), w_gate,
                    w_br_nsa[l].astype(BF16), w_br_ret[l].astype(BF16),
                    w_br_rwkv[l].astype(BF16), w_br_swa[l].astype(BF16),
                    w_out[l].astype(BF16))
        xf = _ffn(xf, row(ffn2_pre_g[l]), row(ffn2_post_g[l]), ffn2_w_gate[l].astype(BF16),
                  ffn2_w_up[l].astype(BF16), ffn2_w_down[l].astype(BF16))
    return xf.reshape(B, S, D)
```

```python
import functools
import math

import numpy as np
import jax
import jax.numpy as jnp
from jax import lax
from jax.experimental import pallas as pl
from jax.experimental.pallas import tpu as pltpu

F32 = jnp.float32
BF16 = jnp.bfloat16

D_MODEL = 1024
HEAD_DIM = 64
HP = 128
ROPE_THETA = 10000.0
RMS_EPS = 1e-6
GN_EPS = 1e-5
RWKV_LN_EPS = 64e-5
D_FF = 2816
NEG_INF = -1e30
FORCED_SCORE = 1e9
LOG2E = math.log2(math.e)

NSA_HEADS = 4
NSA_CMP_LEN = 32
NSA_CMP_STRIDE = 16
NSA_CMP_HIDDEN = 256
NSA_SEL_BLOCK = 64
NSA_TOP_N = 16
NSA_WINDOW = 512
RET_HEADS = 4
RET_DV = 128
RET_CHUNK = 128
RWKV_HEADS = 4
RWKV_CHUNK = 64
SWA_HEADS = 4
SWA_KV_HEADS = 2
SWA_WINDOW = 128
ATT_BLOCK = 128
SLC_TILE = 1024

VMEM_LIMIT = 56 * 1024 * 1024


def _cp(*sem):
    return pltpu.CompilerParams(dimension_semantics=sem, vmem_limit_bytes=VMEM_LIMIT)


def _const_spec(shape):
    nd = len(shape)
    return pl.BlockSpec(shape, lambda *_: (0,) * nd, pipeline_mode=pl.Buffered(1))


def _dot(a, b):
    return jnp.dot(a.astype(BF16), b.astype(BF16), preferred_element_type=F32)


def _dot_nt(a, b):
    return lax.dot_general(a.astype(BF16), b.astype(BF16), (((1,), (1,)), ((), ())),
                           preferred_element_type=F32)


def _dot_tn(a, b):
    return lax.dot_general(a.astype(BF16), b.astype(BF16), (((0,), (0,)), ((), ())),
                           preferred_element_type=F32)


def _rms(x, g):
    return x * lax.rsqrt(jnp.mean(x * x, axis=-1, keepdims=True) + RMS_EPS) * g


def _rope(t, cos, sin_lo, sin_hi):
    return t * cos + pltpu.roll(t, 32, 1) * sin_hi - pltpu.roll(t, HP - 32, 1) * sin_lo


def _ffn_kernel(x_ref, pre_ref, post_ref, wg_ref, wu_ref, wd_ref, o_ref, *, ck):
    x = x_ref[...]
    h = _rms(x, pre_ref[...]).astype(BF16)
    acc = jnp.zeros(x.shape, F32)
    for c in range(D_FF // ck):
        g = jnp.dot(h, wg_ref[:, c * ck:(c + 1) * ck], preferred_element_type=F32)
        u = jnp.dot(h, wu_ref[:, c * ck:(c + 1) * ck], preferred_element_type=F32)
        a = (jax.nn.silu(g) * u).astype(BF16)
        acc = acc + jnp.dot(a, wd_ref[c * ck:(c + 1) * ck, :], preferred_element_type=F32)
    o_ref[...] = x + 0.5 * _rms(acc, post_ref[...])


def _ffn(x, pre_g, post_g, wg, wu, wd, tm=512):
    T, D = x.shape
    return pl.pallas_call(
        functools.partial(_ffn_kernel, ck=256),
        grid=(T // tm,),
        in_specs=[pl.BlockSpec((tm, D), lambda i: (i, 0)),
                  _const_spec((1, D)), _const_spec((1, D)),
                  _const_spec((D, D_FF)), _const_spec((D, D_FF)), _const_spec((D_FF, D))],
        out_specs=pl.BlockSpec((tm, D), lambda i: (i, 0)),
        out_shape=jax.ShapeDtypeStruct((T, D), F32),
        compiler_params=_cp("arbitrary"),
    )(x, pre_g, post_g, wg, wu, wd)


NSA_COLS = 640
RET_COLS = 1536
RW_COLS = 1024
SWA_COLS = 512
RW_OUT = 7 * 4 * HP


def _inproj_kernel(x_ref, pre_ref, rope_ref, w_ref,
                   mu_ref, w0_ref, w2_ref, a0_ref, a2_ref, g2_ref, kk_ref, ka_ref,
                   nq_ref, nqr_ref, nslc_ref, nwin_ref, ncmp_ref,
                   rq_ref, rk_ref, rv_ref, rg_ref, rw_ref, sq_ref, skv_ref,
                   carry_ref, *, tiles_per_seq):
    i = pl.program_id(0)
    x = x_ref[...]
    tm = x.shape[0]
    h = _rms(x, pre_ref[...]).astype(BF16)
    lane = lax.broadcasted_iota(jnp.int32, (tm, HP), 1)
    low = lane < HEAD_DIM
    cos = rope_ref[:, 0:HP]
    s_lo = rope_ref[:, HP:2 * HP]
    s_hi = rope_ref[:, 2 * HP:3 * HP]
    rope = lambda t: _rope(t, cos, s_lo, s_hi)
    rope_k = lambda t: _rope(t, jnp.where(low, cos, 1.0), jnp.where(low, s_lo, 0.0), jnp.where(low, s_hi, 0.0))
    pair = lambda t, j: t[:, j * HP:(j + 1) * HP]
    head = lambda t, hd: jnp.where(low, pair(t, hd // 2) if hd % 2 == 0 else pltpu.roll(pair(t, hd // 2), HEAD_DIM, 1), 0.0)
    head_in_place = lambda t, hd: jnp.where(low if hd % 2 == 0 else ~low, pair(t, hd // 2), 0.0)
    head_swapped = lambda t, hd: jnp.where(~low if hd % 2 == 0 else low, pair(t, hd // 2), 0.0)

    def spread(ref, t, n_heads, col0=0, pick=head):
        for hd in range(n_heads):
            ref[:, col0 + hd * HP:col0 + (hd + 1) * HP] = pick(t, hd).astype(ref.dtype)

    sm_scale = HEAD_DIM ** -0.5 * LOG2E
    proj = lambda c0, c1: jnp.dot(h, w_ref[:, c0:c1], preferred_element_type=F32)

    p = proj(0, NSA_COLS)
    qn = p[:, 0:2 * HP] * sm_scale
    spread(nq_ref, qn, 4)
    spread(nqr_ref, jnp.concatenate([rope(pair(qn, 0)), rope(pair(qn, 1))], axis=1), 4)
    ncmp_ref[...] = pair(p, 2)
    one = (lane == HEAD_DIM).astype(F32)
    slc = rope_k(pair(p, 3))
    nslc_ref[:, 0:HP] = (jnp.where(low, slc, 0.0) + rope_ref[:, 3 * HP:4 * HP]).astype(BF16)
    nslc_ref[:, HP:2 * HP] = (head(slc, 1) + one).astype(BF16)
    win = rope_k(pair(p, 4))
    nwin_ref[:, 0:HP] = head(win, 0).astype(BF16)
    nwin_ref[:, HP:2 * HP] = (head(win, 1) + one).astype(BF16)

    p = proj(NSA_COLS, NSA_COLS + RET_COLS)
    spread(rq_ref, jnp.concatenate([rope(pair(p, 0)), rope(pair(p, 1))], axis=1), 4, pick=head_in_place)
    spread(rk_ref, jnp.concatenate([rope(pair(p, 2)), rope(pair(p, 3))], axis=1) * HEAD_DIM ** -0.5, 4,
           pick=head_in_place)
    rv_ref[...] = p[:, 4 * HP:4 * HP + 512].astype(BF16)
    rg_ref[...] = p[:, 4 * HP + 512:4 * HP + 1024]

    p = proj(NSA_COLS + RET_COLS + RW_COLS, NSA_COLS + RET_COLS + RW_COLS + SWA_COLS)
    spread(sq_ref, jnp.concatenate([rope(pair(p, 0)), rope(pair(p, 1))], axis=1) * sm_scale, 4)
    spread(skv_ref, rope(pair(p, 2)), 2)
    for hd in range(2):
        skv_ref[:, (2 + hd) * HP:(3 + hd) * HP] = (head(pair(p, 3), hd) + one).astype(BF16)

    p = proj(NSA_COLS + RET_COLS, NSA_COLS + RET_COLS + RW_COLS)
    @pl.when(i == 0)
    def _():
        carry_ref[...] = jnp.zeros(carry_ref.shape, F32)

    first = jnp.where(i % tiles_per_seq == 0, 0.0, carry_ref[0:1, :])
    row = lax.broadcasted_iota(jnp.int32, p.shape, 0)
    prev = jnp.where(row == 0, first, pltpu.roll(p, 1, 0))
    carry_ref[0:1, :] = p[tm - 1:tm, :]
    xm = p + mu_ref[...] * (prev - p)
    WC = RWKV_HEADS * HEAD_DIM
    r = xm[:, 0:WC]
    k = xm[:, WC:2 * WC]
    v = xm[:, 2 * WC:3 * WC]
    lora = xm[:, 3 * WC:3 * WC + HP]
    gl = xm[:, 3 * WC + HP:3 * WC + 2 * HP]
    z = -(w0_ref[...] + _dot(jnp.tanh(lora), w2_ref[...]))
    softplus = jnp.maximum(z, 0.0) + jnp.log(1.0 + jnp.exp(-jnp.abs(z)))
    logw = -jnp.exp(-softplus - 0.5)
    a = jax.nn.sigmoid(a0_ref[...] + _dot(lora, a2_ref[...]))
    g = _dot(jax.nn.sigmoid(gl), g2_ref[...])
    kkr = k * kk_ref[...]
    k2 = k * (1.0 + (a - 1.0) * ka_ref[...])
    W4 = 4 * HP
    spread(rw_ref, r, 4, pick=head_in_place)
    spread(rw_ref, k2, 4, col0=W4, pick=head_in_place)
    spread(rw_ref, v, 4, col0=2 * W4, pick=head_swapped)
    for hd in range(4):
        kj = head_in_place(kkr, hd)
        kkj = kj * (1.0 / jnp.maximum(jnp.sqrt(jnp.sum(kj * kj, axis=-1, keepdims=True)), 1e-12))
        rw_ref[:, 3 * W4 + hd * HP:3 * W4 + (hd + 1) * HP] = -kkj
        rw_ref[:, 4 * W4 + hd * HP:4 * W4 + (hd + 1) * HP] = kkj * head_in_place(a, hd)
    spread(rw_ref, logw, 4, col0=5 * W4, pick=head_in_place)
    spread(rw_ref, g, 4, col0=6 * W4, pick=head_swapped)


def _inproj(x, S, pre_g, rope_tab, w, mu, w0, w2, a0, a2, g2, k_k, k_a, tm=256):
    T, D = x.shape
    row = lambda w: pl.BlockSpec((tm, w), lambda i: (i, 0))
    tps = S // tm
    outs = [4 * HP, 4 * HP, 2 * HP, 2 * HP, HP, 4 * HP, 4 * HP, 512, 512, RW_OUT, 4 * HP, 4 * HP]
    dts = [BF16, BF16, BF16, BF16, F32, F32, F32, BF16, F32, F32, BF16, BF16]
    return pl.pallas_call(
        functools.partial(_inproj_kernel, tiles_per_seq=tps),
        grid=(T // tm,),
        in_specs=[row(D), _const_spec((1, D)),
                  pl.BlockSpec((tm, 4 * HP), lambda i: (i % tps, 0)),
                  _const_spec(w.shape),
                  _const_spec(mu.shape), _const_spec(w0.shape), _const_spec(w2.shape), _const_spec(a0.shape),
                  _const_spec(a2.shape), _const_spec(g2.shape), _const_spec(k_k.shape), _const_spec(k_a.shape)],
        out_specs=[row(w) for w in outs],
        out_shape=[jax.ShapeDtypeStruct((T, w), dt) for w, dt in zip(outs, dts)],
        scratch_shapes=[pltpu.VMEM((8, RW_COLS), F32)],
        compiler_params=_cp("arbitrary"),
    )(x, pre_g, rope_tab, w, mu, w0, w2, a0, a2, g2, k_k, k_a)


def _compress_kernel(h_ref, add_ref, w1_ref, w2_ref, kc_ref, vc_ref):
    hb = h_ref[...]
    n_half = hb.shape[0]
    for t, o_ref in ((0, kc_ref), (1, vc_ref)):
        top = _dot(hb + add_ref[2 * t:2 * t + 1, :], w1_ref[2 * t])
        bot = _dot(hb + add_ref[2 * t + 1:2 * t + 2, :], w1_ref[2 * t + 1])
        pre = top + pltpu.roll(bot, n_half - 1, 0)
        o_ref[...] = _dot(jax.nn.gelu(pre), w2_ref[t])


def _compress(hb, add, w1, w2):
    B, n_half, W = hb.shape
    return pl.pallas_call(
        _compress_kernel,
        grid=(B,),
        in_specs=[pl.BlockSpec((None, n_half, W), lambda b: (b, 0, 0)),
                  _const_spec(add.shape), _const_spec(w1.shape), _const_spec(w2.shape)],
        out_specs=[pl.BlockSpec((None, n_half, HP), lambda b: (b, 0, 0))] * 2,
        out_shape=[jax.ShapeDtypeStruct((B, n_half, HP), F32)] * 2,
        compiler_params=_cp("arbitrary"),
    )(hb, add, w1, w2)


def _cmp_kernel(q_ref, kc_ref, vc_ref, ovl_ref, o_ref, pen_ref, *, n_top):
    i = pl.program_id(1)
    tq = q_ref.shape[0]
    n_half = kc_ref.shape[0]
    n_sel = ovl_ref.shape[0]
    kc = kc_ref[...]
    vc = vc_ref[...]
    tpos = i * tq + lax.broadcasted_iota(jnp.int32, (tq, n_half), 0)
    n_id = lax.broadcasted_iota(jnp.int32, (tq, n_half), 1)
    mask = (n_id * NSA_CMP_STRIDE + NSA_CMP_LEN - 1 <= tpos) & (n_id < n_half - 1)
    maskf = mask.astype(F32)
    heads = range(NSA_HEADS)
    s = [jnp.where(mask, _dot_nt(q_ref[:, hd * HP:(hd + 1) * HP], kc), NEG_INF) for hd in heads]
    m = [jnp.max(s[hd], axis=-1, keepdims=True) for hd in heads]
    p = [jnp.exp2(s[hd] - m[hd]) * maskf for hd in heads]
    p = [p[hd] * (1.0 / jnp.maximum(jnp.sum(p[hd], axis=-1, keepdims=True), 1e-30)) for hd in heads]
    o = [_dot(p[hd], vc) for hd in heads]
    imp = [_dot_nt(ovl_ref[...], p[hd]) for hd in heads]
    imp_t = (imp[0] + imp[1]) + (imp[2] + imp[3])
    for hd in heads:
        o_ref[:, hd * HP:(hd + 1) * HP] = o[hd]
    blk = lax.broadcasted_iota(jnp.int32, (n_sel, tq), 0)
    cur = (i * tq + lax.broadcasted_iota(jnp.int32, (n_sel, tq), 1)) // NSA_SEL_BLOCK
    forced = (blk == 0) | (blk == cur) | (blk == cur - 1)
    valid = blk <= cur
    score = jnp.where(forced, FORCED_SCORE, jnp.where(valid, imp_t, -FORCED_SCORE))
    SUB = 8
    grp = [score[g * SUB:(g + 1) * SUB, :] for g in range(n_sel // SUB)]
    rank_g = [jnp.zeros((SUB, tq), F32) for _ in grp]
    sub_row = lax.broadcasted_iota(jnp.int32, (SUB, tq), 0)
    for b in range(n_sel):
        sb = score[b:b + 1, :]
        for g, sc in enumerate(grp):
            if g * SUB > b:
                before = sb >= sc
            elif (g + 1) * SUB - 1 < b:
                before = sb > sc
            else:
                before = (sb > sc) | ((sb == sc) & (sub_row > b - g * SUB))
            rank_g[g] = rank_g[g] + jnp.where(before, 1.0, 0.0)
    rank = jnp.concatenate(rank_g, axis=0)
    pen_t = jnp.where((rank < n_top) & valid, 0.0, NEG_INF)
    pieces = [jnp.zeros((HEAD_DIM, tq), F32), pen_t]
    if HP - HEAD_DIM - n_sel:
        pieces.append(jnp.zeros((HP - HEAD_DIM - n_sel, tq), F32))
    pen_ref[...] = jnp.concatenate(pieces, axis=0).T.astype(BF16)


def _cmp_attn(q, kc, vc, ovl_t, n_top, tq=2 * ATT_BLOCK):
    B, S, W = q.shape
    n_half = kc.shape[1]
    n_sel = ovl_t.shape[0]
    return pl.pallas_call(
        functools.partial(_cmp_kernel, n_top=n_top),
        grid=(B, S // tq),
        in_specs=[pl.BlockSpec((None, tq, W), lambda b, i: (b, i, 0)),
                  pl.BlockSpec((None, n_half, HP), lambda b, i: (b, 0, 0)),
                  pl.BlockSpec((None, n_half, HP), lambda b, i: (b, 0, 0)),
                  _const_spec(ovl_t.shape)],
        out_specs=[pl.BlockSpec((None, tq, W), lambda b, i: (b, i, 0)),
                   pl.BlockSpec((None, tq, HP), lambda b, i: (b, i, 0))],
        out_shape=[jax.ShapeDtypeStruct((B, S, W), F32), jax.ShapeDtypeStruct((B, S, HP), BF16)],
        compiler_params=_cp("arbitrary", "arbitrary"),
    )(q, kc, vc, ovl_t)


def _band_kernel(*refs, n_kv, group, window, back, use_sink):
    if use_sink:
        q_ref, kv_ref, sink_ref, o_ref = refs
    else:
        q_ref, kv_ref, o_ref = refs
    i = pl.program_id(1)
    tq = q_ref.shape[0]
    span = back + tq
    start = pl.multiple_of(jnp.maximum(i * tq - back, 0), ATT_BLOCK)
    tpos = i * tq + lax.broadcasted_iota(jnp.int32, (tq, span), 0)
    diff = tpos - (start + lax.broadcasted_iota(jnp.int32, (tq, span), 1))
    mask = (diff >= 0) & (diff < window)
    heads = range(n_kv * group)
    hsl = lambda hd: slice(hd * HP, (hd + 1) * HP)
    k = [kv_ref[pl.ds(start, span), hsl(kh)] for kh in range(n_kv)]
    v = [kv_ref[pl.ds(start, span), hsl(n_kv + kh)] for kh in range(n_kv)]
    s = [jnp.where(mask, _dot_nt(q_ref[:, hsl(hd)], k[hd // group]), NEG_INF) for hd in heads]
    m = [jnp.max(s[hd], axis=-1, keepdims=True) for hd in heads]
    if use_sink:
        m = [jnp.maximum(m[hd], sink_ref[hd]) for hd in heads]
    p = [jnp.exp2((s[hd] - m[hd]).astype(BF16)) for hd in heads]
    o = [jnp.dot(p[hd], v[hd // group], preferred_element_type=F32) for hd in heads]
    out_lane = lax.broadcasted_iota(jnp.int32, (tq, HP), 1)
    for hd in heads:
        l = jnp.sum(jnp.where(out_lane == HEAD_DIM, o[hd], 0.0), axis=-1, keepdims=True)
        if use_sink:
            l = l + jnp.exp2(sink_ref[hd] - m[hd])
        o_ref[:, hsl(hd)] = jnp.where(out_lane < HEAD_DIM, o[hd] * (1.0 / l), 0.0).astype(o_ref.dtype)


def _band(q, kv, n_kv, group, window, out_dtype, sinks=None, tq=2 * ATT_BLOCK):
    B, S, W = q.shape
    back = -(-(window - 1) // ATT_BLOCK) * ATT_BLOCK
    assert S >= back + tq
    use_sink = sinks is not None
    args = [q, kv]
    in_specs = [pl.BlockSpec((None, tq, W), lambda b, i: (b, i, 0)),
                pl.BlockSpec((None, S, kv.shape[2]), lambda b, i: (b, 0, 0))]
    if use_sink:
        args.append(sinks)
        in_specs.append(pl.BlockSpec(memory_space=pltpu.SMEM))
    return pl.pallas_call(
        functools.partial(_band_kernel, n_kv=n_kv, group=group, window=window, back=back, use_sink=use_sink),
        grid=(B, S // tq),
        in_specs=in_specs,
        out_specs=pl.BlockSpec((None, tq, W), lambda b, i: (b, i, 0)),
        out_shape=jax.ShapeDtypeStruct((B, S, W), out_dtype),
        compiler_params=_cp("arbitrary", "arbitrary"),
    )(*args)


def _slc_kernel(q_ref, kv_ref, pen_ref, o_ref, *, tk):
    i = pl.program_id(1)
    tq = q_ref.shape[0]
    n_t = (i * tq + tq - 1) // tk + 1
    heads = range(NSA_HEADS)
    pen = pen_ref[...].astype(F32)
    q = [(q_ref[:, hd * HP:(hd + 1) * HP].astype(F32) + pen).astype(BF16) for hd in heads]

    def step(j, carry, causal):
        r0 = pl.multiple_of(j * tk, tk)
        k_t = kv_ref[pl.ds(r0, tk), 0:HP]
        v_t = kv_ref[pl.ds(r0, tk), HP:2 * HP]
        s = [_dot_nt(q[hd], k_t) for hd in heads]
        if causal:
            tpos = i * tq + lax.broadcasted_iota(jnp.int32, (tq, tk), 0)
            ok = tpos >= j * tk + lax.broadcasted_iota(jnp.int32, (tq, tk), 1)
            s = [jnp.where(ok, s[hd], NEG_INF) for hd in heads]
        m_new = [jnp.maximum(carry[hd][0], jnp.max(s[hd], axis=-1, keepdims=True)) for hd in heads]
        p = [jnp.exp2((s[hd] - m_new[hd]).astype(BF16)) for hd in heads]
        pv = [jnp.dot(p[hd], v_t, preferred_element_type=F32) for hd in heads]
        return tuple((m_new[hd], carry[hd][1] * jnp.exp2(carry[hd][0] - m_new[hd]) + pv[hd]) for hd in heads)

    init = tuple((jnp.full((tq, 1), NEG_INF, F32), jnp.zeros((tq, HP), F32)) for _ in heads)
    mid = lax.fori_loop(0, n_t - 1, lambda j, c: step(j, c, False), init)
    fin = step(n_t - 1, mid, True)
    out_lane = lax.broadcasted_iota(jnp.int32, (tq, HP), 1)
    for hd in heads:
        acc = fin[hd][1]
        l = jnp.sum(jnp.where(out_lane == HEAD_DIM, acc, 0.0), axis=-1, keepdims=True)
        o_ref[:, hd * HP:(hd + 1) * HP] = jnp.where(out_lane < HEAD_DIM, acc * (1.0 / l), 0.0)


def _slc(q, kv, pen, tq=2 * ATT_BLOCK, tk=SLC_TILE):
    B, S, W = q.shape
    assert tk % tq == 0 and S % tk == 0
    return pl.pallas_call(
        functools.partial(_slc_kernel, tk=tk),
        grid=(B, S // tq),
        in_specs=[pl.BlockSpec((None, tq, W), lambda b, i: (b, i, 0)),
                  pl.BlockSpec((None, S, kv.shape[2]), lambda b, i: (b, 0, 0)),
                  pl.BlockSpec((None, tq, HP), lambda b, i: (b, i, 0))],
        out_specs=pl.BlockSpec((None, tq, W), lambda b, i: (b, i, 0)),
        out_shape=jax.ShapeDtypeStruct((B, S, W), F32),
        compiler_params=_cp("arbitrary", "arbitrary"),
    )(q, kv, pen)


def _ret_kernel(q_ref, k_ref, v_ref, g_ref, dm_ref, zeta_ref, xi_ref, gch_ref, gn_ref, o_ref, st_ref, *, n_chunks):
    c = pl.program_id(1)

    @pl.when(c == 0)
    def _():
        st_ref[...] = jnp.zeros(st_ref.shape, F32)

    C = RET_CHUNK
    pairs = [(ch, hd) for ch in range(n_chunks) for hd in range(RET_HEADS)]
    rows = lambda ch: slice(ch * C, (ch + 1) * C)
    q = {p: q_ref[rows(p[0]), p[1] * HP:(p[1] + 1) * HP] for p in pairs}
    k = {p: k_ref[rows(p[0]), p[1] * HP:(p[1] + 1) * HP] for p in pairs}
    v = {p: v_ref[rows(p[0]), p[1] * RET_DV:(p[1] + 1) * RET_DV] for p in pairs}
    inner = {p: _dot_nt(q[p], k[p]) * dm_ref[p[1]] for p in pairs}
    o = {p: _dot(inner[p], v[p]) for p in pairs}
    kv = {p: _dot_tn(k[p] * zeta_ref[p[1]], v[p]) for p in pairs}
    for hd in range(RET_HEADS):
        state = st_ref[hd]
        for ch in range(n_chunks):
            p = (ch, hd)
            o[p] = o[p] + _dot(q[p] * xi_ref[hd], state)
            state = state * gch_ref[hd] + kv[p]
        st_ref[hd] = state
    for p in pairs:
        ch, hd = p
        mu = jnp.mean(o[p], axis=-1, keepdims=True)
        d = o[p] - mu
        var = jnp.mean(d * d, axis=-1, keepdims=True)
        on = d * lax.rsqrt(var + GN_EPS) * gn_ref[:, hd * RET_DV:(hd + 1) * RET_DV]
        gate = jax.nn.silu(g_ref[rows(ch), hd * RET_DV:(hd + 1) * RET_DV])
        o_ref[rows(ch), hd * RET_DV:(hd + 1) * RET_DV] = (gate * on).astype(BF16)


def _retention(q, k, v, g, dmask, zeta, xi, gch, gn_g, n_chunks=4):
    B, S, _ = q.shape
    C = n_chunks * RET_CHUNK
    blk = lambda w: pl.BlockSpec((None, C, w), lambda b, c: (b, c, 0))
    return pl.pallas_call(
        functools.partial(_ret_kernel, n_chunks=n_chunks),
        grid=(B, S // C),
        in_specs=[blk(4 * HP), blk(4 * HP), blk(512), blk(512),
                  _const_spec(dmask.shape), _const_spec(zeta.shape), _const_spec(xi.shape),
                  _const_spec(gch.shape), _const_spec(gn_g.shape)],
        out_specs=blk(512),
        out_shape=jax.ShapeDtypeStruct((B, S, 512), BF16),
        scratch_shapes=[pltpu.VMEM((RET_HEADS, HP, RET_DV), F32)],
        compiler_params=_cp("arbitrary", "arbitrary"),
    )(q, k, v, g, dmask, zeta, xi, gch, gn_g)


def _rwkv_kernel(rw_ref, rk_ref, lng_ref, lnb_ref, o_ref, z_ref, *, n_chunks):
    c = pl.program_id(1)

    @pl.when(c == 0)
    def _():
        z_ref[...] = jnp.zeros(z_ref.shape, F32)

    L = RWKV_CHUNK
    W4 = 4 * HP
    ri = lax.broadcasted_iota(jnp.int32, (2 * L, 2 * L), 0)
    ci = lax.broadcasted_iota(jnp.int32, (2 * L, 2 * L), 1) % L
    a_mask = ((ri < L) & (ri > ci)) | (ri - L >= ci)
    tri = (lax.broadcasted_iota(jnp.int32, (L, L), 0) >= lax.broadcasted_iota(jnp.int32, (L, L), 1)).astype(BF16)
    dr = lax.broadcasted_iota(jnp.int32, (HP, HP), 0)
    dc = lax.broadcasted_iota(jnp.int32, (HP, HP), 1)
    lane_lo = lax.broadcasted_iota(jnp.int32, (L, HP), 1) < HEAD_DIM
    key = [(lane_lo if hd % 2 == 0 else ~lane_lo).astype(F32) for hd in range(RWKV_HEADS)]
    val = [1.0 - m for m in key]
    col_key = [((dc < HEAD_DIM) if hd % 2 == 0 else (dc >= HEAD_DIM)).astype(F32) for hd in range(RWKV_HEADS)]
    col_val = [1.0 - m for m in col_key]
    zeros = jnp.zeros((L, HP), F32)
    tdot = lambda t: jnp.dot(tri, t, preferred_element_type=F32)
    pairs = [(ch, hd) for ch in range(n_chunks) for hd in range(RWKV_HEADS)]
    col = lambda ch, hd, gi: rw_ref[ch * L:(ch + 1) * L, gi * W4 + hd * HP:gi * W4 + (hd + 1) * HP]
    cum_all = []
    for ch in range(n_chunks):
        lw_all = rw_ref[ch * L:(ch + 1) * L, 5 * W4:6 * W4]
        hi = lw_all.astype(BF16)
        rem = lw_all - hi.astype(F32)
        mid = rem.astype(BF16)
        lo = (rem - mid.astype(F32)).astype(BF16)
        cum_all.append(tdot(hi) + tdot(mid) + tdot(lo))
    cum = {p: cum_all[p[0]][:, p[1] * HP:(p[1] + 1) * HP] for p in pairs}
    r, k, v, a, b, lw = ({p: col(p[0], p[1], gi) for p in pairs} for gi in range(6))
    e_inv = {p: jnp.exp(-cum[p]) for p in pairs}
    at = {p: a[p] * jnp.exp(cum[p] - lw[p]) for p in pairs}
    rt = {p: r[p] * jnp.exp(cum[p]) for p in pairs}
    amat = {p: jnp.where(a_mask, _dot_nt(jnp.concatenate([at[p], rt[p]], axis=0),
                                         jnp.concatenate([b[p] * e_inv[p], k[p] * e_inv[p]], axis=0)), 0.0)
            for p in pairs}
    x = {p: at[p] + _dot(amat[p][0:L], jnp.concatenate([zeros, v[p]], axis=0)) for p in pairs}
    npow = {p: amat[p][0:L, 0:L] for p in pairs}
    for s in range(int(math.log2(L))):
        x = {p: x[p] + _dot(npow[p], x[p]) for p in pairs}
        if (2 << s) < L:
            npow = {p: _dot(npow[p], npow[p]) for p in pairs}
    rhs = {p: jnp.concatenate([x[p], v[p]], axis=0) for p in pairs}
    yq = {p: _dot(amat[p][L:2 * L], rhs[p]) for p in pairs}
    mn = {}
    for p in pairs:
        e_rem = jnp.exp(cum[p][L - 1:L, :] - cum[p])
        mn[p] = _dot_tn(jnp.concatenate([b[p] * e_rem, k[p] * e_rem], axis=0), rhs[p])
    y = {}
    for hd in range(RWKV_HEADS):
        z = z_ref[hd]
        for ch in range(n_chunks):
            p = (ch, hd)
            y[p] = yq[p] * val[hd] + _dot(rt[p] + yq[p] * key[hd], z)
            decay = jnp.where(dr == dc, jnp.exp(cum[p][L - 1:L, :]), 0.0)
            z = _dot(decay + mn[p] * col_key[hd], z) + mn[p] * col_val[hd]
        z_ref[hd] = z
    for p in pairs:
        ch, hd = p
        mu = jnp.sum(y[p], axis=-1, keepdims=True) * (1.0 / HEAD_DIM)
        d = (y[p] - mu) * val[hd]
        var = jnp.sum(d * d, axis=-1, keepdims=True) * (1.0 / HEAD_DIM)
        yn = d * lax.rsqrt(var + RWKV_LN_EPS) * lng_ref[:, hd * HP:(hd + 1) * HP] + lnb_ref[:, hd * HP:(hd + 1) * HP]
        bonus = jnp.sum(r[p] * k[p] * rk_ref[:, hd * HP:(hd + 1) * HP], axis=-1, keepdims=True) * v[p]
        o_ref[ch * L:(ch + 1) * L, hd * HP:(hd + 1) * HP] = ((yn + bonus) * col(ch, hd, 6)).astype(o_ref.dtype)


def _rwkv(rw, r_k, ln_g, ln_b, n_chunks=8):
    B, S, _ = rw.shape
    L = n_chunks * RWKV_CHUNK
    return pl.pallas_call(
        functools.partial(_rwkv_kernel, n_chunks=n_chunks),
        grid=(B, S // L),
        in_specs=[pl.BlockSpec((None, L, RW_OUT), lambda b, c: (b, c, 0)),
                  _const_spec(r_k.shape), _const_spec(ln_g.shape), _const_spec(ln_b.shape)],
        out_specs=pl.BlockSpec((None, L, 4 * HP), lambda b, c: (b, c, 0)),
        out_shape=jax.ShapeDtypeStruct((B, S, 4 * HP), BF16),
        scratch_shapes=[pltpu.VMEM((RWKV_HEADS, HP, HP), F32)],
        compiler_params=_cp("arbitrary", "arbitrary"),
    )(rw, r_k, ln_g, ln_b)


def _merge_kernel(x_ref, ocmp_ref, oslc_ref, owin_ref, yret_ref, yrw_ref, yswa_ref,
                  pre_ref, post_ref, wg_ref, wn_ref, wr_ref, ww_ref, ws_ref, wo_ref, o_ref):
    x = x_ref[...]
    D = x.shape[1]
    h = _rms(x, pre_ref[...]).astype(BF16)
    tm = x.shape[0]
    gate = lambda j, w: jax.nn.sigmoid(jnp.dot(h, wg_ref[:, j:j + w], preferred_element_type=F32))
    g_nsa = gate(4 * D, HP)
    bc = lambda j: jnp.broadcast_to(g_nsa[:, j:j + 1], (tm, HP))
    hsl = lambda hd: slice(hd * HP, (hd + 1) * HP)
    y_nsa = [bc(3 * hd) * ocmp_ref[:, hsl(hd)] + bc(3 * hd + 1) * oslc_ref[:, hsl(hd)]
             + bc(3 * hd + 2) * owin_ref[:, hsl(hd)] for hd in range(NSA_HEADS)]
    low = lax.broadcasted_iota(jnp.int32, (tm, HP), 1) < HEAD_DIM
    pack = lambda hs: jnp.concatenate([jnp.where(low, hs[j], pltpu.roll(hs[j + 1], HEAD_DIM, 1))
                                       for j in range(0, len(hs), 2)], axis=1)
    heads_of = lambda ref: [ref[:, hsl(hd)].astype(F32) for hd in range(ref.shape[1] // HP)]
    pack_in_place = lambda hs: jnp.concatenate([jnp.where(low, hs[j + 1], hs[j]) for j in range(0, len(hs), 2)], axis=1)
    merged = gate(0, D) * _dot(pack(y_nsa), wn_ref[...])
    merged = merged + gate(D, D) * _dot(yret_ref[...], wr_ref[...])
    merged = merged + gate(2 * D, D) * _dot(pack_in_place(heads_of(yrw_ref)), ww_ref[...])
    merged = merged + gate(3 * D, D) * _dot(pack(heads_of(yswa_ref)), ws_ref[...])
    o_ref[...] = x + _rms(_dot(merged, wo_ref[...]), post_ref[...])


def _merge(x, ocmp, oslc, owin, yret, yrw, yswa, pre_g, post_g, wg, wn, wr, ww, ws, wo, tm=256):
    T, D = x.shape
    row = lambda w: pl.BlockSpec((tm, w), lambda i: (i, 0))
    return pl.pallas_call(
        _merge_kernel,
        grid=(T // tm,),
        in_specs=[row(D), row(4 * HP), row(4 * HP), row(4 * HP), row(512), row(4 * HP), row(4 * HP),
                  _const_spec((1, D)), _const_spec((1, D)), _const_spec(wg.shape), _const_spec(wn.shape),
                  _const_spec(wr.shape), _const_spec(ww.shape), _const_spec(ws.shape), _const_spec(wo.shape)],
        out_specs=row(D),
        out_shape=jax.ShapeDtypeStruct((T, D), F32),
        compiler_params=_cp("arbitrary"),
    )(x, ocmp, oslc, owin, yret, yrw, yswa, pre_g, post_g, wg, wn, wr, ww, ws, wo)


def _column_layout():
    spec = (
        ('nsa_q', 256), ('nsa_k_cmp', 64), ('nsa_v_cmp', 64), ('nsa_k_slc', 64), ('nsa_v_slc', 64),
        ('nsa_k_win', 64), ('nsa_v_win', 64), ('nsa_gate', 12),
        ('ret_q', 256), ('ret_k', 256), ('ret_v', 512), ('ret_g', 512),
        ('rwkv', 1024),
        ('swa_q', 256), ('swa_k', 128), ('swa_v', 128),
        ('branch_gate', 4 * D_MODEL),
    )
    layout, start = {}, 0
    for name, width in spec:
        layout[name] = (start, start + width)
        start += width
    return layout


def _pad_heads_in_place(w, value_half=False):
    w = w.reshape(-1, 2, HEAD_DIM)
    z = jnp.zeros_like(w[:, 0])
    lo_hi = lambda t, lower: jnp.concatenate([t, z] if lower else [z, t], axis=1)
    return jnp.stack([lo_hi(w[:, 0], not value_half), lo_hi(w[:, 1], value_half)], axis=1).reshape(-1)


def _swap_head_pairs(w, axis):
    w = jnp.moveaxis(w, axis, -1)
    shape = w.shape
    w = w.reshape(shape[:-1] + (-1, 2, HEAD_DIM))[..., ::-1, :].reshape(shape)
    return jnp.moveaxis(w, -1, axis)


def _pad_cols(w, width):
    return jnp.pad(w, [(0, 0)] * (w.ndim - 1) + [(0, width - w.shape[-1])])


def _rope_table(S):
    half = HEAD_DIM // 2
    inv_freq = jnp.power(ROPE_THETA, -jnp.arange(half, dtype=F32) * 2.0 / HEAD_DIM)
    ang = jnp.arange(S, dtype=jnp.int32).astype(F32)[:, None] * inv_freq[None, :]
    cos, sin, z = jnp.cos(ang), jnp.sin(ang), jnp.zeros((S, half), F32)
    assert S // NSA_SEL_BLOCK <= HP - HEAD_DIM
    blk = jnp.asarray(np.arange(S)[:, None] // NSA_SEL_BLOCK == np.arange(HP)[None, :] - HEAD_DIM, F32)
    return jnp.concatenate([cos, cos, cos, cos, sin, z, sin, z, z, sin, z, sin, blk], axis=1)


def _retention_tables():
    H, C = RET_HEADS, RET_CHUNK
    log_g = jnp.log(1.0 - jnp.power(2.0, -5.0 - jnp.arange(H, dtype=F32)))
    i = jnp.arange(C, dtype=F32)
    diff = i[:, None] - i[None, :]
    dmask = jnp.where(diff >= 0, jnp.exp(log_g[:, None, None] * jnp.maximum(diff, 0.0)), 0.0)
    zeta = jnp.exp(log_g[:, None] * (C - 1 - i)[None, :])
    xi = jnp.exp(log_g[:, None] * (i + 1.0)[None, :])
    gch = jnp.exp(log_g * C)
    bc = lambda t: jnp.broadcast_to(t[:, :, None], (H, C, HP))
    return dmask, bc(zeta), bc(xi), jnp.broadcast_to(gch[:, None, None], (H, HP, RET_DV))


def kernel(x, ffn1_pre_g, ffn1_post_g, ffn1_w_gate, ffn1_w_up, ffn1_w_down, mix_pre_g, mix_post_g, w_in, nsa_cmp_pos_k, nsa_cmp_pos_v, nsa_cmp_k_w1, nsa_cmp_k_w2, nsa_cmp_v_w1, nsa_cmp_v_w2, ret_gn_g, rwkv_mu, rwkv_w0, rwkv_w2, rwkv_a0, rwkv_a2, rwkv_g2, rwkv_k_k, rwkv_k_a, rwkv_r_k, rwkv_ln_g, rwkv_ln_b, swa_sinks, w_br_nsa, w_br_ret, w_br_rwkv, w_br_swa, w_out, ffn2_pre_g, ffn2_post_g, ffn2_w_gate, ffn2_w_up, ffn2_w_down):
    B, S, D = x.shape
    T = B * S
    depth = w_in.shape[0]
    lay = _column_layout()
    n_half = S // NSA_CMP_STRIDE
    n_sel = S // NSA_SEL_BLOCK
    n_top = min(NSA_TOP_N, n_sel)
    rope_tab = _rope_table(S)
    dmask, zeta, xi, gch = _retention_tables()
    cs = np.arange(n_half) * NSA_CMP_STRIDE
    ss = np.arange(n_sel) * NSA_SEL_BLOCK
    ovl = ((cs[None, :] <= ss[:, None] + NSA_SEL_BLOCK - 1) & (cs[None, :] + NSA_CMP_LEN - 1 >= ss[:, None])
           & (np.arange(n_half)[None, :] < n_half - 1))
    ovl_t = jnp.asarray(ovl, BF16)
    row = lambda t: t.reshape(1, -1)

    xf = x.reshape(T, D)
    for l in range(depth):
        wcols = lambda a, b: w_in[l, :, a:b].astype(BF16)
        c = lambda name: wcols(*lay[name])
        xf = _ffn(xf, row(ffn1_pre_g[l]), row(ffn1_post_g[l]), ffn1_w_gate[l].astype(BF16),
                  ffn1_w_up[l].astype(BF16), ffn1_w_down[l].astype(BF16))
        v0, v1 = lay['rwkv'][0] + 2 * RWKV_HEADS * HEAD_DIM, lay['rwkv'][0] + 3 * RWKV_HEADS * HEAD_DIM
        w_mix = jnp.concatenate([wcols(lay['nsa_q'][0], lay['nsa_gate'][0]), wcols(lay['ret_q'][0], v0),
                                 _swap_head_pairs(wcols(v0, v1), 1), wcols(v1, lay['branch_gate'][0])], axis=1)
        mu = rwkv_mu[l]
        mu = jnp.concatenate([mu[:v0 - lay['rwkv'][0]], _swap_head_pairs(mu[v0 - lay['rwkv'][0]:v1 - lay['rwkv'][0]], 0),
                              mu[v1 - lay['rwkv'][0]:]])
        w2_p = jnp.pad(rwkv_w2[l], ((0, HP - rwkv_w2.shape[1]), (0, 0)))
        a2_p = jnp.pad(rwkv_a2[l], ((HP - rwkv_a2.shape[1], 0), (0, 0)))
        (nsa_q, nsa_qr, kv_slc, kv_win, kv_cmp, ret_q, ret_k, ret_v, ret_g, rw, swa_q, swa_kv) = _inproj(
            xf, S, row(mix_pre_g[l]), rope_tab, w_mix, row(mu),
            row(rwkv_w0[l]), w2_p, row(rwkv_a0[l]), a2_p, _swap_head_pairs(rwkv_g2[l], 1), row(rwkv_k_k[l]),
            row(rwkv_k_a[l]))
        b3 = lambda t: t.reshape(B, S, t.shape[-1])

        half = NSA_CMP_STRIDE * HEAD_DIM
        z16 = jnp.zeros((NSA_CMP_STRIDE, HEAD_DIM), F32)
        emb = lambda pk, pv: jnp.concatenate([pk, pv], axis=1).reshape(1, -1)
        add = jnp.concatenate([emb(nsa_cmp_pos_k[l][:16], z16), emb(nsa_cmp_pos_k[l][16:], z16),
                               emb(z16, nsa_cmp_pos_v[l][:16]), emb(z16, nsa_cmp_pos_v[l][16:])], axis=0)

        def w1_embed(w1_half, is_v):
            w = w1_half.reshape(NSA_CMP_STRIDE, HEAD_DIM, NSA_CMP_HIDDEN)
            zz = jnp.zeros_like(w)
            parts = (zz, w) if is_v else (w, zz)
            return jnp.concatenate(parts, axis=1).reshape(2 * half, NSA_CMP_HIDDEN)

        w1 = jnp.stack([w1_embed(nsa_cmp_k_w1[l][:half], False), w1_embed(nsa_cmp_k_w1[l][half:], False),
                        w1_embed(nsa_cmp_v_w1[l][:half], True), w1_embed(nsa_cmp_v_w1[l][half:], True)]).astype(BF16)
        w2 = jnp.stack([_pad_cols(nsa_cmp_k_w2[l], HP), _pad_cols(nsa_cmp_v_w2[l], HP)]).astype(BF16)
        kc, vc = _compress(kv_cmp.reshape(B, n_half, NSA_CMP_STRIDE * 2 * HEAD_DIM), add, w1, w2)
        o_cmp, pen = _cmp_attn(b3(nsa_q), kc, vc, ovl_t, n_top)
        o_slc = _slc(b3(nsa_qr), b3(kv_slc), pen)
        o_win = _band(b3(nsa_qr), b3(kv_win), 1, NSA_HEADS, NSA_WINDOW, F32)
        y_ret = _retention(b3(ret_q), b3(ret_k), b3(ret_v), b3(ret_g), dmask, zeta, xi, gch, row(ret_gn_g[l]))
        y_rw = _rwkv(b3(rw), row(_pad_heads_in_place(rwkv_r_k[l])),
                     row(_pad_heads_in_place(rwkv_ln_g[l], value_half=True)),
                     row(_pad_heads_in_place(rwkv_ln_b[l], value_half=True)))
        y_swa = _band(b3(swa_q), b3(swa_kv), SWA_KV_HEADS, SWA_HEADS // SWA_KV_HEADS, SWA_WINDOW, BF16,
                      sinks=swa_sinks[l] * LOG2E)
        w_gate = jnp.concatenate([c('branch_gate'), _pad_cols(c('nsa_gate'), HP)], axis=1)
        f2 = lambda t: t.reshape(T, t.shape[-1])
        xf = _merge(xf, f2(o_cmp), f2(o_slc), f2(o_win), f2(y_ret), f2(y_rw), f2(y_swa),
                    row(mix_pre_g[l]), row(mix_post_g[l]), w_gate,
                    w_br_nsa[l].astype(BF16), w_br_ret[l].astype(BF16),
                    _swap_head_pairs(w_br_rwkv[l], 0).astype(BF16), w_br_swa[l].astype(BF16),
                    w_out[l].astype(BF16))
        xf = _ffn(xf, row(ffn2_pre_g[l]), row(ffn2_post_g[l]), ffn2_w_gate[l].astype(BF16),
                  ffn2_w_up[l].astype(BF16), ffn2_w_down[l].astype(BF16))
    return xf.reshape(B, S, D)
```

```python
import functools
import math

import numpy as np
import jax
import jax.numpy as jnp
from jax import lax
from jax.experimental import pallas as pl
from jax.experimental.pallas import tpu as pltpu

F32 = jnp.float32
BF16 = jnp.bfloat16

D_MODEL = 1024
HEAD_DIM = 64
HP = 128
ROPE_THETA = 10000.0
RMS_EPS = 1e-6
GN_EPS = 1e-5
RWKV_LN_EPS = 64e-5
D_FF = 2816
NEG_INF = -1e30
FORCED_SCORE = 1e9
LOG2E = math.log2(math.e)

NSA_HEADS = 4
NSA_CMP_LEN = 32
NSA_CMP_STRIDE = 16
NSA_CMP_HIDDEN = 256
NSA_SEL_BLOCK = 64
NSA_TOP_N = 16
NSA_WINDOW = 512
RET_HEADS = 4
RET_DV = 128
RET_CHUNK = 128
RWKV_HEADS = 4
RWKV_CHUNK = 64
SWA_HEADS = 4
SWA_KV_HEADS = 2
SWA_WINDOW = 128
ATT_BLOCK = 128
SLC_TILE = 1024

VMEM_LIMIT = 56 * 1024 * 1024


def _cp(*sem):
    return pltpu.CompilerParams(dimension_semantics=sem, vmem_limit_bytes=VMEM_LIMIT)


def _const_spec(shape):
    nd = len(shape)
    return pl.BlockSpec(shape, lambda *_: (0,) * nd, pipeline_mode=pl.Buffered(1))


def _dot(a, b):
    return jnp.dot(a.astype(BF16), b.astype(BF16), preferred_element_type=F32)


def _dot_nt(a, b):
    return lax.dot_general(a.astype(BF16), b.astype(BF16), (((1,), (1,)), ((), ())),
                           preferred_element_type=F32)


def _dot_tn(a, b):
    return lax.dot_general(a.astype(BF16), b.astype(BF16), (((0,), (0,)), ((), ())),
                           preferred_element_type=F32)


def _rms(x, g):
    return x * lax.rsqrt(jnp.mean(x * x, axis=-1, keepdims=True) + RMS_EPS) * g


def _rope(t, cos, sin_lo, sin_hi):
    return t * cos + pltpu.roll(t, 32, 1) * sin_hi - pltpu.roll(t, HP - 32, 1) * sin_lo


def _ffn_kernel(x_ref, pre_ref, post_ref, wg_ref, wu_ref, wd_ref, o_ref, *, ck):
    x = x_ref[...]
    h = _rms(x, pre_ref[...]).astype(BF16)
    acc = jnp.zeros(x.shape, F32)
    for c in range(D_FF // ck):
        g = jnp.dot(h, wg_ref[:, c * ck:(c + 1) * ck], preferred_element_type=F32)
        u = jnp.dot(h, wu_ref[:, c * ck:(c + 1) * ck], preferred_element_type=F32)
        a = (jax.nn.silu(g) * u).astype(BF16)
        acc = acc + jnp.dot(a, wd_ref[c * ck:(c + 1) * ck, :], preferred_element_type=F32)
    o_ref[...] = x + 0.5 * _rms(acc, post_ref[...])


def _ffn(x, pre_g, post_g, wg, wu, wd, tm=512):
    T, D = x.shape
    return pl.pallas_call(
        functools.partial(_ffn_kernel, ck=256),
        grid=(T // tm,),
        in_specs=[pl.BlockSpec((tm, D), lambda i: (i, 0)),
                  _const_spec((1, D)), _const_spec((1, D)),
                  _const_spec((D, D_FF)), _const_spec((D, D_FF)), _const_spec((D_FF, D))],
        out_specs=pl.BlockSpec((tm, D), lambda i: (i, 0)),
        out_shape=jax.ShapeDtypeStruct((T, D), F32),
        compiler_params=_cp("arbitrary"),
    )(x, pre_g, post_g, wg, wu, wd)


NSA_COLS = 640
RET_COLS = 1536
RW_COLS = 1024
SWA_COLS = 512
RW_OUT = 7 * 4 * HP


def _inproj_kernel(x_ref, pre_ref, rope_ref, w_ref,
                   mu_ref, w0_ref, w2_ref, a0_ref, a2_ref, g2_ref, kk_ref, ka_ref,
                   nq_ref, nqr_ref, nslc_ref, nwin_ref, ncmp_ref,
                   rq_ref, rk_ref, rv_ref, rg_ref, rw_ref, sq_ref, skv_ref,
                   carry_ref, *, tiles_per_seq):
    i = pl.program_id(0)
    x = x_ref[...]
    tm = x.shape[0]
    h = _rms(x, pre_ref[...]).astype(BF16)
    lane = lax.broadcasted_iota(jnp.int32, (tm, HP), 1)
    low = lane < HEAD_DIM
    cos = rope_ref[:, 0:HP]
    s_lo = rope_ref[:, HP:2 * HP]
    s_hi = rope_ref[:, 2 * HP:3 * HP]
    rope = lambda t: _rope(t, cos, s_lo, s_hi)
    rope_k = lambda t: _rope(t, jnp.where(low, cos, 1.0), jnp.where(low, s_lo, 0.0), jnp.where(low, s_hi, 0.0))
    pair = lambda t, j: t[:, j * HP:(j + 1) * HP]
    head = lambda t, hd: jnp.where(low, pair(t, hd // 2) if hd % 2 == 0 else pltpu.roll(pair(t, hd // 2), HEAD_DIM, 1), 0.0)
    head_in_place = lambda t, hd: jnp.where(low if hd % 2 == 0 else ~low, pair(t, hd // 2), 0.0)
    head_swapped = lambda t, hd: jnp.where(~low if hd % 2 == 0 else low, pair(t, hd // 2), 0.0)

    def spread(ref, t, n_heads, col0=0, pick=head):
        for hd in range(n_heads):
            ref[:, col0 + hd * HP:col0 + (hd + 1) * HP] = pick(t, hd).astype(ref.dtype)

    sm_scale = HEAD_DIM ** -0.5 * LOG2E
    proj = lambda c0, c1: jnp.dot(h, w_ref[:, c0:c1], preferred_element_type=F32)

    p = proj(0, NSA_COLS)
    qn = p[:, 0:2 * HP] * sm_scale
    spread(nq_ref, qn, 4)
    spread(nqr_ref, jnp.concatenate([rope(pair(qn, 0)), rope(pair(qn, 1))], axis=1), 4)
    ncmp_ref[...] = pair(p, 2)
    one = (lane == HEAD_DIM).astype(F32)
    slc = rope_k(pair(p, 3))
    nslc_ref[:, 0:HP] = (jnp.where(low, slc, 0.0) + rope_ref[:, 3 * HP:4 * HP]).astype(BF16)
    nslc_ref[:, HP:2 * HP] = (head(slc, 1) + one).astype(BF16)
    win = rope_k(pair(p, 4))
    nwin_ref[:, 0:HP] = head(win, 0).astype(BF16)
    nwin_ref[:, HP:2 * HP] = (head(win, 1) + one).astype(BF16)

    p = proj(NSA_COLS, NSA_COLS + RET_COLS)
    spread(rq_ref, jnp.concatenate([rope(pair(p, 0)), rope(pair(p, 1))], axis=1), 4, pick=head_in_place)
    spread(rk_ref, jnp.concatenate([rope(pair(p, 2)), rope(pair(p, 3))], axis=1) * HEAD_DIM ** -0.5, 4,
           pick=head_in_place)
    rv_ref[...] = p[:, 4 * HP:4 * HP + 512].astype(BF16)
    rg_ref[...] = p[:, 4 * HP + 512:4 * HP + 1024]

    p = proj(NSA_COLS + RET_COLS + RW_COLS, NSA_COLS + RET_COLS + RW_COLS + SWA_COLS)
    spread(sq_ref, jnp.concatenate([rope(pair(p, 0)), rope(pair(p, 1))], axis=1) * sm_scale, 4)
    spread(skv_ref, rope(pair(p, 2)), 2)
    for hd in range(2):
        skv_ref[:, (2 + hd) * HP:(3 + hd) * HP] = (head(pair(p, 3), hd) + one).astype(BF16)

    p = proj(NSA_COLS + RET_COLS, NSA_COLS + RET_COLS + RW_COLS)
    @pl.when(i == 0)
    def _():
        carry_ref[...] = jnp.zeros(carry_ref.shape, F32)

    first = jnp.where(i % tiles_per_seq == 0, 0.0, carry_ref[0:1, :])
    row = lax.broadcasted_iota(jnp.int32, p.shape, 0)
    prev = jnp.where(row == 0, first, pltpu.roll(p, 1, 0))
    carry_ref[0:1, :] = p[tm - 1:tm, :]
    xm = p + mu_ref[...] * (prev - p)
    WC = RWKV_HEADS * HEAD_DIM
    r = xm[:, 0:WC]
    k = xm[:, WC:2 * WC]
    v = xm[:, 2 * WC:3 * WC]
    lora = xm[:, 3 * WC:3 * WC + HP]
    gl = xm[:, 3 * WC + HP:3 * WC + 2 * HP]
    z = -(w0_ref[...] + _dot(jnp.tanh(lora), w2_ref[...]))
    softplus = jnp.maximum(z, 0.0) + jnp.log(1.0 + jnp.exp(-jnp.abs(z)))
    logw = -jnp.exp(-softplus - 0.5)
    a = jax.nn.sigmoid(a0_ref[...] + _dot(lora, a2_ref[...]))
    g = _dot(jax.nn.sigmoid(gl), g2_ref[...])
    kkr = k * kk_ref[...]
    k2 = k * (1.0 + (a - 1.0) * ka_ref[...])
    W4 = 4 * HP
    spread(rw_ref, r, 4, pick=head_in_place)
    spread(rw_ref, k2, 4, col0=W4, pick=head_in_place)
    spread(rw_ref, v, 4, col0=2 * W4, pick=head_swapped)
    for hd in range(4):
        kj = head_in_place(kkr, hd)
        kkj = kj * (1.0 / jnp.maximum(jnp.sqrt(jnp.sum(kj * kj, axis=-1, keepdims=True)), 1e-12))
        rw_ref[:, 3 * W4 + hd * HP:3 * W4 + (hd + 1) * HP] = -kkj
        rw_ref[:, 4 * W4 + hd * HP:4 * W4 + (hd + 1) * HP] = kkj * head_in_place(a, hd)
    spread(rw_ref, logw, 4, col0=5 * W4, pick=head_in_place)
    spread(rw_ref, g, 4, col0=6 * W4, pick=head_swapped)


def _inproj(x, S, pre_g, rope_tab, w, mu, w0, w2, a0, a2, g2, k_k, k_a, tm=256):
    T, D = x.shape
    row = lambda w: pl.BlockSpec((tm, w), lambda i: (i, 0))
    tps = S // tm
    outs = [4 * HP, 4 * HP, 2 * HP, 2 * HP, HP, 4 * HP, 4 * HP, 512, 512, RW_OUT, 4 * HP, 4 * HP]
    dts = [BF16, BF16, BF16, BF16, F32, F32, F32, BF16, F32, F32, BF16, BF16]
    return pl.pallas_call(
        functools.partial(_inproj_kernel, tiles_per_seq=tps),
        grid=(T // tm,),
        in_specs=[row(D), _const_spec((1, D)),
                  pl.BlockSpec((tm, 4 * HP), lambda i: (i % tps, 0)),
                  _const_spec(w.shape),
                  _const_spec(mu.shape), _const_spec(w0.shape), _const_spec(w2.shape), _const_spec(a0.shape),
                  _const_spec(a2.shape), _const_spec(g2.shape), _const_spec(k_k.shape), _const_spec(k_a.shape)],
        out_specs=[row(w) for w in outs],
        out_shape=[jax.ShapeDtypeStruct((T, w), dt) for w, dt in zip(outs, dts)],
        scratch_shapes=[pltpu.VMEM((8, RW_COLS), F32)],
        compiler_params=_cp("arbitrary"),
    )(x, pre_g, rope_tab, w, mu, w0, w2, a0, a2, g2, k_k, k_a)


def _compress_kernel(h_ref, add_ref, w1_ref, w2_ref, kc_ref, vc_ref):
    hb = h_ref[...]
    n_half = hb.shape[0]
    for t, o_ref in ((0, kc_ref), (1, vc_ref)):
        top = _dot(hb + add_ref[2 * t:2 * t + 1, :], w1_ref[2 * t])
        bot = _dot(hb + add_ref[2 * t + 1:2 * t + 2, :], w1_ref[2 * t + 1])
        pre = top + pltpu.roll(bot, n_half - 1, 0)
        o_ref[...] = _dot(jax.nn.gelu(pre), w2_ref[t])


def _compress(hb, add, w1, w2):
    B, n_half, W = hb.shape
    return pl.pallas_call(
        _compress_kernel,
        grid=(B,),
        in_specs=[pl.BlockSpec((None, n_half, W), lambda b: (b, 0, 0)),
                  _const_spec(add.shape), _const_spec(w1.shape), _const_spec(w2.shape)],
        out_specs=[pl.BlockSpec((None, n_half, HP), lambda b: (b, 0, 0))] * 2,
        out_shape=[jax.ShapeDtypeStruct((B, n_half, HP), F32)] * 2,
        compiler_params=_cp("arbitrary"),
    )(hb, add, w1, w2)


def _cmp_kernel(q_ref, kc_ref, vc_ref, ovl_ref, o_ref, pen_ref, *, n_top):
    i = pl.program_id(1)
    tq = q_ref.shape[0]
    n_half = kc_ref.shape[0]
    n_sel = ovl_ref.shape[0]
    kc = kc_ref[...]
    vc = vc_ref[...]
    tpos = i * tq + lax.broadcasted_iota(jnp.int32, (tq, n_half), 0)
    n_id = lax.broadcasted_iota(jnp.int32, (tq, n_half), 1)
    mask = (n_id * NSA_CMP_STRIDE + NSA_CMP_LEN - 1 <= tpos) & (n_id < n_half - 1)
    maskf = mask.astype(F32)
    heads = range(NSA_HEADS)
    s = [jnp.where(mask, _dot_nt(q_ref[:, hd * HP:(hd + 1) * HP], kc), NEG_INF) for hd in heads]
    m = [jnp.max(s[hd], axis=-1, keepdims=True) for hd in heads]
    p = [jnp.exp2(s[hd] - m[hd]) * maskf for hd in heads]
    p = [p[hd] * (1.0 / jnp.maximum(jnp.sum(p[hd], axis=-1, keepdims=True), 1e-30)) for hd in heads]
    o = [_dot(p[hd], vc) for hd in heads]
    imp = [_dot_nt(ovl_ref[...], p[hd]) for hd in heads]
    imp_t = (imp[0] + imp[1]) + (imp[2] + imp[3])
    for hd in heads:
        o_ref[:, hd * HP:(hd + 1) * HP] = o[hd]
    blk = lax.broadcasted_iota(jnp.int32, (n_sel, tq), 0)
    cur = (i * tq + lax.broadcasted_iota(jnp.int32, (n_sel, tq), 1)) // NSA_SEL_BLOCK
    forced = (blk == 0) | (blk == cur) | (blk == cur - 1)
    valid = blk <= cur
    score = jnp.where(forced, FORCED_SCORE, jnp.where(valid, imp_t, -FORCED_SCORE))
    SUB = 8
    grp = [score[g * SUB:(g + 1) * SUB, :] for g in range(n_sel // SUB)]
    rank_g = [jnp.zeros((SUB, tq), F32) for _ in grp]
    sub_row = lax.broadcasted_iota(jnp.int32, (SUB, tq), 0)
    for b in range(n_sel):
        sb = score[b:b + 1, :]
        for g, sc in enumerate(grp):
            if g * SUB > b:
                before = sb >= sc
            elif (g + 1) * SUB - 1 < b:
                before = sb > sc
            else:
                before = (sb > sc) | ((sb == sc) & (sub_row > b - g * SUB))
            rank_g[g] = rank_g[g] + jnp.where(before, 1.0, 0.0)
    rank = jnp.concatenate(rank_g, axis=0)
    pen_t = jnp.where((rank < n_top) & valid, 0.0, NEG_INF)
    pieces = [jnp.zeros((HEAD_DIM, tq), F32), pen_t]
    if HP - HEAD_DIM - n_sel:
        pieces.append(jnp.zeros((HP - HEAD_DIM - n_sel, tq), F32))
    pen_ref[...] = jnp.concatenate(pieces, axis=0).T.astype(BF16)


def _cmp_attn(q, kc, vc, ovl_t, n_top, tq=2 * ATT_BLOCK):
    B, S, W = q.shape
    n_half = kc.shape[1]
    n_sel = ovl_t.shape[0]
    return pl.pallas_call(
        functools.partial(_cmp_kernel, n_top=n_top),
        grid=(B, S // tq),
        in_specs=[pl.BlockSpec((None, tq, W), lambda b, i: (b, i, 0)),
                  pl.BlockSpec((None, n_half, HP), lambda b, i: (b, 0, 0)),
                  pl.BlockSpec((None, n_half, HP), lambda b, i: (b, 0, 0)),
                  _const_spec(ovl_t.shape)],
        out_specs=[pl.BlockSpec((None, tq, W), lambda b, i: (b, i, 0)),
                   pl.BlockSpec((None, tq, HP), lambda b, i: (b, i, 0))],
        out_shape=[jax.ShapeDtypeStruct((B, S, W), F32), jax.ShapeDtypeStruct((B, S, HP), BF16)],
        compiler_params=_cp("arbitrary", "arbitrary"),
    )(q, kc, vc, ovl_t)


def _band_kernel(*refs, n_kv, group, window, back, use_sink):
    if use_sink:
        q_ref, kv_ref, sink_ref, o_ref = refs
    else:
        q_ref, kv_ref, o_ref = refs
    i = pl.program_id(1)
    tq = q_ref.shape[0]
    span = back + tq
    start = pl.multiple_of(jnp.maximum(i * tq - back, 0), ATT_BLOCK)
    tpos = i * tq + lax.broadcasted_iota(jnp.int32, (tq, span), 0)
    diff = tpos - (start + lax.broadcasted_iota(jnp.int32, (tq, span), 1))
    mask = (diff >= 0) & (diff < window)
    heads = range(n_kv * group)
    hsl = lambda hd: slice(hd * HP, (hd + 1) * HP)
    k = [kv_ref[pl.ds(start, span), hsl(kh)] for kh in range(n_kv)]
    v = [kv_ref[pl.ds(start, span), hsl(n_kv + kh)] for kh in range(n_kv)]
    s = [jnp.where(mask, _dot_nt(q_ref[:, hsl(hd)], k[hd // group]), NEG_INF) for hd in heads]
    m = [jnp.max(s[hd], axis=-1, keepdims=True) for hd in heads]
    if use_sink:
        m = [jnp.maximum(m[hd], sink_ref[hd]) for hd in heads]
    p = [jnp.exp2((s[hd] - m[hd]).astype(BF16)) for hd in heads]
    o = [jnp.dot(p[hd], v[hd // group], preferred_element_type=F32) for hd in heads]
    out_lane = lax.broadcasted_iota(jnp.int32, (tq, HP), 1)
    for hd in heads:
        l = jnp.sum(jnp.where(out_lane == HEAD_DIM, o[hd], 0.0), axis=-1, keepdims=True)
        if use_sink:
            l = l + jnp.exp2(sink_ref[hd] - m[hd])
        o_ref[:, hsl(hd)] = jnp.where(out_lane < HEAD_DIM, o[hd] * (1.0 / l), 0.0).astype(o_ref.dtype)


def _band(q, kv, n_kv, group, window, out_dtype, sinks=None, tq=2 * ATT_BLOCK):
    B, S, W = q.shape
    back = -(-(window - 1) // ATT_BLOCK) * ATT_BLOCK
    assert S >= back + tq
    use_sink = sinks is not None
    args = [q, kv]
    in_specs = [pl.BlockSpec((None, tq, W), lambda b, i: (b, i, 0)),
                pl.BlockSpec((None, S, kv.shape[2]), lambda b, i: (b, 0, 0))]
    if use_sink:
        args.append(sinks)
        in_specs.append(pl.BlockSpec(memory_space=pltpu.SMEM))
    return pl.pallas_call(
        functools.partial(_band_kernel, n_kv=n_kv, group=group, window=window, back=back, use_sink=use_sink),
        grid=(B, S // tq),
        in_specs=in_specs,
        out_specs=pl.BlockSpec((None, tq, W), lambda b, i: (b, i, 0)),
        out_shape=jax.ShapeDtypeStruct((B, S, W), out_dtype),
        compiler_params=_cp("arbitrary", "arbitrary"),
    )(*args)


def _slc_kernel(q_ref, kv_ref, pen_ref, o_ref, *, tk):
    i = pl.program_id(1)
    tq = q_ref.shape[0]
    n_t = (i * tq + tq - 1) // tk + 1
    heads = range(NSA_HEADS)
    pen = pen_ref[...].astype(F32)
    q = [(q_ref[:, hd * HP:(hd + 1) * HP].astype(F32) + pen).astype(BF16) for hd in heads]

    def step(j, carry, causal):
        r0 = pl.multiple_of(j * tk, tk)
        k_t = kv_ref[pl.ds(r0, tk), 0:HP]
        v_t = kv_ref[pl.ds(r0, tk), HP:2 * HP]
        s = [_dot_nt(q[hd], k_t) for hd in heads]
        if causal:
            tpos = i * tq + lax.broadcasted_iota(jnp.int32, (tq, tk), 0)
            ok = tpos >= j * tk + lax.broadcasted_iota(jnp.int32, (tq, tk), 1)
            s = [jnp.where(ok, s[hd], NEG_INF) for hd in heads]
        m_new = [jnp.maximum(carry[hd][0], jnp.max(s[hd], axis=-1, keepdims=True)) for hd in heads]
        p = [jnp.exp2((s[hd] - m_new[hd]).astype(BF16)) for hd in heads]
        pv = [jnp.dot(p[hd], v_t, preferred_element_type=F32) for hd in heads]
        return tuple((m_new[hd], carry[hd][1] * jnp.exp2(carry[hd][0] - m_new[hd]) + pv[hd]) for hd in heads)

    init = tuple((jnp.full((tq, 1), NEG_INF, F32), jnp.zeros((tq, HP), F32)) for _ in heads)
    mid = lax.fori_loop(0, n_t - 1, lambda j, c: step(j, c, False), init)
    fin = step(n_t - 1, mid, True)
    out_lane = lax.broadcasted_iota(jnp.int32, (tq, HP), 1)
    for hd in heads:
        acc = fin[hd][1]
        l = jnp.sum(jnp.where(out_lane == HEAD_DIM, acc, 0.0), axis=-1, keepdims=True)
        o_ref[:, hd * HP:(hd + 1) * HP] = jnp.where(out_lane < HEAD_DIM, acc * (1.0 / l), 0.0)


def _slc(q, kv, pen, tq=4 * ATT_BLOCK, tk=SLC_TILE):
    B, S, W = q.shape
    assert tk % tq == 0 and S % tk == 0
    return pl.pallas_call(
        functools.partial(_slc_kernel, tk=tk),
        grid=(B, S // tq),
        in_specs=[pl.BlockSpec((None, tq, W), lambda b, i: (b, i, 0)),
                  pl.BlockSpec((None, S, kv.shape[2]), lambda b, i: (b, 0, 0)),
                  pl.BlockSpec((None, tq, HP), lambda b, i: (b, i, 0))],
        out_specs=pl.BlockSpec((None, tq, W), lambda b, i: (b, i, 0)),
        out_shape=jax.ShapeDtypeStruct((B, S, W), F32),
        compiler_params=_cp("arbitrary", "arbitrary"),
    )(q, kv, pen)


def _ret_kernel(q_ref, k_ref, v_ref, g_ref, dm_ref, zeta_ref, xi_ref, gch_ref, gn_ref, o_ref, st_ref, *, n_chunks):
    c = pl.program_id(1)

    @pl.when(c == 0)
    def _():
        st_ref[...] = jnp.zeros(st_ref.shape, F32)

    C = RET_CHUNK
    pairs = [(ch, hd) for ch in range(n_chunks) for hd in range(RET_HEADS)]
    rows = lambda ch: slice(ch * C, (ch + 1) * C)
    q = {p: q_ref[rows(p[0]), p[1] * HP:(p[1] + 1) * HP] for p in pairs}
    k = {p: k_ref[rows(p[0]), p[1] * HP:(p[1] + 1) * HP] for p in pairs}
    v = {p: v_ref[rows(p[0]), p[1] * RET_DV:(p[1] + 1) * RET_DV] for p in pairs}
    inner = {p: _dot_nt(q[p], k[p]) * dm_ref[p[1]] for p in pairs}
    o = {p: _dot(inner[p], v[p]) for p in pairs}
    kv = {p: _dot_tn(k[p] * zeta_ref[p[1]], v[p]) for p in pairs}
    for hd in range(RET_HEADS):
        state = st_ref[hd]
        for ch in range(n_chunks):
            p = (ch, hd)
            o[p] = o[p] + _dot(q[p] * xi_ref[hd], state)
            state = state * gch_ref[hd] + kv[p]
        st_ref[hd] = state
    for p in pairs:
        ch, hd = p
        mu = jnp.mean(o[p], axis=-1, keepdims=True)
        d = o[p] - mu
        var = jnp.mean(d * d, axis=-1, keepdims=True)
        on = d * lax.rsqrt(var + GN_EPS) * gn_ref[:, hd * RET_DV:(hd + 1) * RET_DV]
        gate = jax.nn.silu(g_ref[rows(ch), hd * RET_DV:(hd + 1) * RET_DV])
        o_ref[rows(ch), hd * RET_DV:(hd + 1) * RET_DV] = (gate * on).astype(BF16)


def _retention(q, k, v, g, dmask, zeta, xi, gch, gn_g, n_chunks=4):
    B, S, _ = q.shape
    C = n_chunks * RET_CHUNK
    blk = lambda w: pl.BlockSpec((None, C, w), lambda b, c: (b, c, 0))
    return pl.pallas_call(
        functools.partial(_ret_kernel, n_chunks=n_chunks),
        grid=(B, S // C),
        in_specs=[blk(4 * HP), blk(4 * HP), blk(512), blk(512),
                  _const_spec(dmask.shape), _const_spec(zeta.shape), _const_spec(xi.shape),
                  _const_spec(gch.shape), _const_spec(gn_g.shape)],
        out_specs=blk(512),
        out_shape=jax.ShapeDtypeStruct((B, S, 512), BF16),
        scratch_shapes=[pltpu.VMEM((RET_HEADS, HP, RET_DV), F32)],
        compiler_params=_cp("arbitrary", "arbitrary"),
    )(q, k, v, g, dmask, zeta, xi, gch, gn_g)


def _rwkv_kernel(rw_ref, rk_ref, lng_ref, lnb_ref, o_ref, z_ref, *, n_chunks):
    c = pl.program_id(1)

    @pl.when(c == 0)
    def _():
        z_ref[...] = jnp.zeros(z_ref.shape, F32)

    L = RWKV_CHUNK
    W4 = 4 * HP
    ri = lax.broadcasted_iota(jnp.int32, (2 * L, 2 * L), 0)
    ci = lax.broadcasted_iota(jnp.int32, (2 * L, 2 * L), 1) % L
    a_mask = ((ri < L) & (ri > ci)) | (ri - L >= ci)
    tri = (lax.broadcasted_iota(jnp.int32, (L, L), 0) >= lax.broadcasted_iota(jnp.int32, (L, L), 1)).astype(BF16)
    dr = lax.broadcasted_iota(jnp.int32, (HP, HP), 0)
    dc = lax.broadcasted_iota(jnp.int32, (HP, HP), 1)
    lane_lo = lax.broadcasted_iota(jnp.int32, (L, HP), 1) < HEAD_DIM
    key = [(lane_lo if hd % 2 == 0 else ~lane_lo).astype(F32) for hd in range(RWKV_HEADS)]
    val = [1.0 - m for m in key]
    col_key = [((dc < HEAD_DIM) if hd % 2 == 0 else (dc >= HEAD_DIM)).astype(F32) for hd in range(RWKV_HEADS)]
    col_val = [1.0 - m for m in col_key]
    zeros = jnp.zeros((L, HP), F32)
    tdot = lambda t: jnp.dot(tri, t, preferred_element_type=F32)
    pairs = [(ch, hd) for ch in range(n_chunks) for hd in range(RWKV_HEADS)]
    col = lambda ch, hd, gi: rw_ref[ch * L:(ch + 1) * L, gi * W4 + hd * HP:gi * W4 + (hd + 1) * HP]
    cum_all = []
    for ch in range(n_chunks):
        lw_all = rw_ref[ch * L:(ch + 1) * L, 5 * W4:6 * W4]
        hi = lw_all.astype(BF16)
        rem = lw_all - hi.astype(F32)
        mid = rem.astype(BF16)
        lo = (rem - mid.astype(F32)).astype(BF16)
        cum_all.append(tdot(hi) + tdot(mid) + tdot(lo))
    cum = {p: cum_all[p[0]][:, p[1] * HP:(p[1] + 1) * HP] for p in pairs}
    r, k, v, a, b, lw = ({p: col(p[0], p[1], gi) for p in pairs} for gi in range(6))
    e_inv = {p: jnp.exp(-cum[p]) for p in pairs}
    at = {p: a[p] * jnp.exp(cum[p] - lw[p]) for p in pairs}
    rt = {p: r[p] * jnp.exp(cum[p]) for p in pairs}
    amat = {p: jnp.where(a_mask, _dot_nt(jnp.concatenate([at[p], rt[p]], axis=0),
                                         jnp.concatenate([b[p] * e_inv[p], k[p] * e_inv[p]], axis=0)), 0.0)
            for p in pairs}
    x = {p: at[p] + _dot(amat[p][0:L], jnp.concatenate([zeros, v[p]], axis=0)) for p in pairs}
    npow = {p: amat[p][0:L, 0:L] for p in pairs}
    for s in range(int(math.log2(L))):
        x = {p: x[p] + _dot(npow[p], x[p]) for p in pairs}
        if (2 << s) < L:
            npow = {p: _dot(npow[p], npow[p]) for p in pairs}
    rhs = {p: jnp.concatenate([x[p], v[p]], axis=0) for p in pairs}
    yq = {p: _dot(amat[p][L:2 * L], rhs[p]) for p in pairs}
    mn = {}
    for p in pairs:
        e_rem = jnp.exp(cum[p][L - 1:L, :] - cum[p])
        mn[p] = _dot_tn(jnp.concatenate([b[p] * e_rem, k[p] * e_rem], axis=0), rhs[p])
    y = {}
    for hd in range(RWKV_HEADS):
        z = z_ref[hd]
        for ch in range(n_chunks):
            p = (ch, hd)
            y[p] = yq[p] * val[hd] + _dot(rt[p] + yq[p] * key[hd], z)
            decay = jnp.where(dr == dc, jnp.exp(cum[p][L - 1:L, :]), 0.0)
            z = _dot(decay + mn[p] * col_key[hd], z) + mn[p] * col_val[hd]
        z_ref[hd] = z
    for p in pairs:
        ch, hd = p
        mu = jnp.sum(y[p], axis=-1, keepdims=True) * (1.0 / HEAD_DIM)
        d = (y[p] - mu) * val[hd]
        var = jnp.sum(d * d, axis=-1, keepdims=True) * (1.0 / HEAD_DIM)
        yn = d * lax.rsqrt(var + RWKV_LN_EPS) * lng_ref[:, hd * HP:(hd + 1) * HP] + lnb_ref[:, hd * HP:(hd + 1) * HP]
        bonus = jnp.sum(r[p] * k[p] * rk_ref[:, hd * HP:(hd + 1) * HP], axis=-1, keepdims=True) * v[p]
        o_ref[ch * L:(ch + 1) * L, hd * HP:(hd + 1) * HP] = ((yn + bonus) * col(ch, hd, 6)).astype(o_ref.dtype)


def _rwkv(rw, r_k, ln_g, ln_b, n_chunks=8):
    B, S, _ = rw.shape
    L = n_chunks * RWKV_CHUNK
    return pl.pallas_call(
        functools.partial(_rwkv_kernel, n_chunks=n_chunks),
        grid=(B, S // L),
        in_specs=[pl.BlockSpec((None, L, RW_OUT), lambda b, c: (b, c, 0)),
                  _const_spec(r_k.shape), _const_spec(ln_g.shape), _const_spec(ln_b.shape)],
        out_specs=pl.BlockSpec((None, L, 4 * HP), lambda b, c: (b, c, 0)),
        out_shape=jax.ShapeDtypeStruct((B, S, 4 * HP), BF16),
        scratch_shapes=[pltpu.VMEM((RWKV_HEADS, HP, HP), F32)],
        compiler_params=_cp("arbitrary", "arbitrary"),
    )(rw, r_k, ln_g, ln_b)


def _merge_kernel(x_ref, ocmp_ref, oslc_ref, owin_ref, yret_ref, yrw_ref, yswa_ref,
                  pre_ref, post_ref, wg_ref, wn_ref, wr_ref, ww_ref, ws_ref, wo_ref, o_ref):
    x = x_ref[...]
    D = x.shape[1]
    h = _rms(x, pre_ref[...]).astype(BF16)
    tm = x.shape[0]
    gate = lambda j, w: jax.nn.sigmoid(jnp.dot(h, wg_ref[:, j:j + w], preferred_element_type=F32))
    g_nsa = gate(4 * D, HP)
    bc = lambda j: jnp.broadcast_to(g_nsa[:, j:j + 1], (tm, HP))
    hsl = lambda hd: slice(hd * HP, (hd + 1) * HP)
    y_nsa = [bc(3 * hd) * ocmp_ref[:, hsl(hd)] + bc(3 * hd + 1) * oslc_ref[:, hsl(hd)]
             + bc(3 * hd + 2) * owin_ref[:, hsl(hd)] for hd in range(NSA_HEADS)]
    low = lax.broadcasted_iota(jnp.int32, (tm, HP), 1) < HEAD_DIM
    pack = lambda hs: jnp.concatenate([jnp.where(low, hs[j], pltpu.roll(hs[j + 1], HEAD_DIM, 1))
                                       for j in range(0, len(hs), 2)], axis=1)
    heads_of = lambda ref: [ref[:, hsl(hd)].astype(F32) for hd in range(ref.shape[1] // HP)]
    pack_in_place = lambda hs: jnp.concatenate([jnp.where(low, hs[j + 1], hs[j]) for j in range(0, len(hs), 2)], axis=1)
    merged = gate(0, D) * _dot(pack(y_nsa), wn_ref[...])
    merged = merged + gate(D, D) * _dot(yret_ref[...], wr_ref[...])
    merged = merged + gate(2 * D, D) * _dot(pack_in_place(heads_of(yrw_ref)), ww_ref[...])
    merged = merged + gate(3 * D, D) * _dot(pack(heads_of(yswa_ref)), ws_ref[...])
    o_ref[...] = x + _rms(_dot(merged, wo_ref[...]), post_ref[...])


def _merge(x, ocmp, oslc, owin, yret, yrw, yswa, pre_g, post_g, wg, wn, wr, ww, ws, wo, tm=512):
    T, D = x.shape
    row = lambda w: pl.BlockSpec((tm, w), lambda i: (i, 0))
    return pl.pallas_call(
        _merge_kernel,
        grid=(T // tm,),
        in_specs=[row(D), row(4 * HP), row(4 * HP), row(4 * HP), row(512), row(4 * HP), row(4 * HP),
                  _const_spec((1, D)), _const_spec((1, D)), _const_spec(wg.shape), _const_spec(wn.shape),
                  _const_spec(wr.shape), _const_spec(ww.shape), _const_spec(ws.shape), _const_spec(wo.shape)],
        out_specs=row(D),
        out_shape=jax.ShapeDtypeStruct((T, D), F32),
        compiler_params=_cp("arbitrary"),
    )(x, ocmp, oslc, owin, yret, yrw, yswa, pre_g, post_g, wg, wn, wr, ww, ws, wo)


def _column_layout():
    spec = (
        ('nsa_q', 256), ('nsa_k_cmp', 64), ('nsa_v_cmp', 64), ('nsa_k_slc', 64), ('nsa_v_slc', 64),
        ('nsa_k_win', 64), ('nsa_v_win', 64), ('nsa_gate', 12),
        ('ret_q', 256), ('ret_k', 256), ('ret_v', 512), ('ret_g', 512),
        ('rwkv', 1024),
        ('swa_q', 256), ('swa_k', 128), ('swa_v', 128),
        ('branch_gate', 4 * D_MODEL),
    )
    layout, start = {}, 0
    for name, width in spec:
        layout[name] = (start, start + width)
        start += width
    return layout


def _pad_heads_in_place(w, value_half=False):
    w = w.reshape(-1, 2, HEAD_DIM)
    z = jnp.zeros_like(w[:, 0])
    lo_hi = lambda t, lower: jnp.concatenate([t, z] if lower else [z, t], axis=1)
    return jnp.stack([lo_hi(w[:, 0], not value_half), lo_hi(w[:, 1], value_half)], axis=1).reshape(-1)


def _swap_head_pairs(w, axis):
    w = jnp.moveaxis(w, axis, -1)
    shape = w.shape
    w = w.reshape(shape[:-1] + (-1, 2, HEAD_DIM))[..., ::-1, :].reshape(shape)
    return jnp.moveaxis(w, -1, axis)


def _pad_cols(w, width):
    return jnp.pad(w, [(0, 0)] * (w.ndim - 1) + [(0, width - w.shape[-1])])


def _rope_table(S):
    half = HEAD_DIM // 2
    inv_freq = jnp.power(ROPE_THETA, -jnp.arange(half, dtype=F32) * 2.0 / HEAD_DIM)
    ang = jnp.arange(S, dtype=jnp.int32).astype(F32)[:, None] * inv_freq[None, :]
    cos, sin, z = jnp.cos(ang), jnp.sin(ang), jnp.zeros((S, half), F32)
    assert S // NSA_SEL_BLOCK <= HP - HEAD_DIM
    blk = jnp.asarray(np.arange(S)[:, None] // NSA_SEL_BLOCK == np.arange(HP)[None, :] - HEAD_DIM, F32)
    return jnp.concatenate([cos, cos, cos, cos, sin, z, sin, z, z, sin, z, sin, blk], axis=1)


def _retention_tables():
    H, C = RET_HEADS, RET_CHUNK
    log_g = jnp.log(1.0 - jnp.power(2.0, -5.0 - jnp.arange(H, dtype=F32)))
    i = jnp.arange(C, dtype=F32)
    diff = i[:, None] - i[None, :]
    dmask = jnp.where(diff >= 0, jnp.exp(log_g[:, None, None] * jnp.maximum(diff, 0.0)), 0.0)
    zeta = jnp.exp(log_g[:, None] * (C - 1 - i)[None, :])
    xi = jnp.exp(log_g[:, None] * (i + 1.0)[None, :])
    gch = jnp.exp(log_g * C)
    bc = lambda t: jnp.broadcast_to(t[:, :, None], (H, C, HP))
    return dmask, bc(zeta), bc(xi), jnp.broadcast_to(gch[:, None, None], (H, HP, RET_DV))


def kernel(x, ffn1_pre_g, ffn1_post_g, ffn1_w_gate, ffn1_w_up, ffn1_w_down, mix_pre_g, mix_post_g, w_in, nsa_cmp_pos_k, nsa_cmp_pos_v, nsa_cmp_k_w1, nsa_cmp_k_w2, nsa_cmp_v_w1, nsa_cmp_v_w2, ret_gn_g, rwkv_mu, rwkv_w0, rwkv_w2, rwkv_a0, rwkv_a2, rwkv_g2, rwkv_k_k, rwkv_k_a, rwkv_r_k, rwkv_ln_g, rwkv_ln_b, swa_sinks, w_br_nsa, w_br_ret, w_br_rwkv, w_br_swa, w_out, ffn2_pre_g, ffn2_post_g, ffn2_w_gate, ffn2_w_up, ffn2_w_down):
    B, S, D = x.shape
    T = B * S
    depth = w_in.shape[0]
    lay = _column_layout()
    n_half = S // NSA_CMP_STRIDE
    n_sel = S // NSA_SEL_BLOCK
    n_top = min(NSA_TOP_N, n_sel)
    rope_tab = _rope_table(S)
    dmask, zeta, xi, gch = _retention_tables()
    cs = np.arange(n_half) * NSA_CMP_STRIDE
    ss = np.arange(n_sel) * NSA_SEL_BLOCK
    ovl = ((cs[None, :] <= ss[:, None] + NSA_SEL_BLOCK - 1) & (cs[None, :] + NSA_CMP_LEN - 1 >= ss[:, None])
           & (np.arange(n_half)[None, :] < n_half - 1))
    ovl_t = jnp.asarray(ovl, BF16)
    row = lambda t: t.reshape(1, -1)

    xf = x.reshape(T, D)
    for l in range(depth):
        wcols = lambda a, b: w_in[l, :, a:b].astype(BF16)
        c = lambda name: wcols(*lay[name])
        xf = _ffn(xf, row(ffn1_pre_g[l]), row(ffn1_post_g[l]), ffn1_w_gate[l].astype(BF16),
                  ffn1_w_up[l].astype(BF16), ffn1_w_down[l].astype(BF16))
        v0, v1 = lay['rwkv'][0] + 2 * RWKV_HEADS * HEAD_DIM, lay['rwkv'][0] + 3 * RWKV_HEADS * HEAD_DIM
        w_mix = jnp.concatenate([wcols(lay['nsa_q'][0], lay['nsa_gate'][0]), wcols(lay['ret_q'][0], v0),
                                 _swap_head_pairs(wcols(v0, v1), 1), wcols(v1, lay['branch_gate'][0])], axis=1)
        mu = rwkv_mu[l]
        mu = jnp.concatenate([mu[:v0 - lay['rwkv'][0]], _swap_head_pairs(mu[v0 - lay['rwkv'][0]:v1 - lay['rwkv'][0]], 0),
                              mu[v1 - lay['rwkv'][0]:]])
        w2_p = jnp.pad(rwkv_w2[l], ((0, HP - rwkv_w2.shape[1]), (0, 0)))
        a2_p = jnp.pad(rwkv_a2[l], ((HP - rwkv_a2.shape[1], 0), (0, 0)))
        (nsa_q, nsa_qr, kv_slc, kv_win, kv_cmp, ret_q, ret_k, ret_v, ret_g, rw, swa_q, swa_kv) = _inproj(
            xf, S, row(mix_pre_g[l]), rope_tab, w_mix, row(mu),
            row(rwkv_w0[l]), w2_p, row(rwkv_a0[l]), a2_p, _swap_head_pairs(rwkv_g2[l], 1), row(rwkv_k_k[l]),
            row(rwkv_k_a[l]))
        b3 = lambda t: t.reshape(B, S, t.shape[-1])

        half = NSA_CMP_STRIDE * HEAD_DIM
        z16 = jnp.zeros((NSA_CMP_STRIDE, HEAD_DIM), F32)
        emb = lambda pk, pv: jnp.concatenate([pk, pv], axis=1).reshape(1, -1)
        add = jnp.concatenate([emb(nsa_cmp_pos_k[l][:16], z16), emb(nsa_cmp_pos_k[l][16:], z16),
                               emb(z16, nsa_cmp_pos_v[l][:16]), emb(z16, nsa_cmp_pos_v[l][16:])], axis=0)

        def w1_embed(w1_half, is_v):
            w = w1_half.reshape(NSA_CMP_STRIDE, HEAD_DIM, NSA_CMP_HIDDEN)
            zz = jnp.zeros_like(w)
            parts = (zz, w) if is_v else (w, zz)
            return jnp.concatenate(parts, axis=1).reshape(2 * half, NSA_CMP_HIDDEN)

        w1 = jnp.stack([w1_embed(nsa_cmp_k_w1[l][:half], False), w1_embed(nsa_cmp_k_w1[l][half:], False),
                        w1_embed(nsa_cmp_v_w1[l][:half], True), w1_embed(nsa_cmp_v_w1[l][half:], True)]).astype(BF16)
        w2 = jnp.stack([_pad_cols(nsa_cmp_k_w2[l], HP), _pad_cols(nsa_cmp_v_w2[l], HP)]).astype(BF16)
        kc, vc = _compress(kv_cmp.reshape(B, n_half, NSA_CMP_STRIDE * 2 * HEAD_DIM), add, w1, w2)
        o_cmp, pen = _cmp_attn(b3(nsa_q), kc, vc, ovl_t, n_top)
        o_slc = _slc(b3(nsa_qr), b3(kv_slc), pen)
        o_win = _band(b3(nsa_qr), b3(kv_win), 1, NSA_HEADS, NSA_WINDOW, F32)
        y_ret = _retention(b3(ret_q), b3(ret_k), b3(ret_v), b3(ret_g), dmask, zeta, xi, gch, row(ret_gn_g[l]))
        y_rw = _rwkv(b3(rw), row(_pad_heads_in_place(rwkv_r_k[l])),
                     row(_pad_heads_in_place(rwkv_ln_g[l], value_half=True)),
                     row(_pad_heads_in_place(rwkv_ln_b[l], value_half=True)))
        y_swa = _band(b3(swa_q), b3(swa_kv), SWA_KV_HEADS, SWA_HEADS // SWA_KV_HEADS, SWA_WINDOW, BF16,
                      sinks=swa_sinks[l] * LOG2E)
        w_gate = jnp.concatenate([c('branch_gate'), _pad_cols(c('nsa_gate'), HP)], axis=1)
        f2 = lambda t: t.reshape(T, t.shape[-1])
        xf = _merge(xf, f2(o_cmp), f2(o_slc), f2(o_win), f2(y_ret), f2(y_rw), f2(y_swa),
                    row(mix_pre_g[l]), row(mix_post_g[l]), w_gate,
                    w_br_nsa[l].astype(BF16), w_br_ret[l].astype(BF16),
                    _swap_head_pairs(w_br_rwkv[l], 0).astype(BF16), w_br_swa[l].astype(BF16),
                    w_out[l].astype(BF16))
        xf = _ffn(xf, row(ffn2_pre_g[l]), row(ffn2_post_g[l]), ffn2_w_gate[l].astype(BF16),
                  ffn2_w_up[l].astype(BF16), ffn2_w_down[l].astype(BF16))
    return xf.reshape(B, S, D)
```
